```python
import math
import jax, jax.numpy as jnp
from jax import lax
import numpy as np

D_MODEL = 1024
BATCH = 8
SEQ = 4096
DEPTH = 2
DEC_BATCH = 32
DEC_SEQ = 1
PAST_LEN = 16384
PAGE_SIZE = 128

HEAD_DIM = 64
DIL_GROUPS = ((128, 1), (512, 4), (2048, 16))
HEADS_PER_GROUP = 8
N_DIL_HEADS = HEADS_PER_GROUP * len(DIL_GROUPS)
DIL_QKV = N_DIL_HEADS * HEAD_DIM
DIL_OUT = HEADS_PER_GROUP * HEAD_DIM
DIL_BLOCK = 128
RET_HEADS = 4
RET_DK = 128
RET_DV = 256
RET_QK = RET_HEADS * RET_DK
RET_V = RET_HEADS * RET_DV
RET_CHUNK = 128
D_FF = 2816
N_IN = 3 * DIL_QKV + 2 * RET_QK + 2 * RET_V + 2 * D_MODEL
NORM_EPS = 1e-6
GN_EPS = 1e-6

kernel_name = 'hybrid_dilated_retention_macaron_step'


def rmsnorm(x, g):
    x32 = x.astype(jnp.float32)
    y = x32 * lax.rsqrt(jnp.mean(x32 * x32, axis=-1, keepdims=True) + NORM_EPS)
    return (y * g.astype(jnp.float32)).astype(x.dtype)


def swiglu(x, wg, wu, wd):
    return (jax.nn.silu(x @ wg) * (x @ wu)) @ wd


def alibi_slopes():
    return 2.0 ** (-8.0 * jnp.arange(1, N_DIL_HEADS + 1, dtype=jnp.float32) / N_DIL_HEADS)


def dilated_prompt(q, k, v, dil, n_back, slopes):
    B, S, H, E = q.shape
    unit = dil * DIL_BLOCK
    s_pad = -(-S // unit) * unit
    nb = s_pad // unit

    def blocks(t):
        t = jnp.pad(t, ((0, 0), (0, s_pad - S), (0, 0), (0, 0)))
        return t.reshape(B, nb, DIL_BLOCK, dil, H, E)

    def with_prev(t):
        prev = jnp.pad(t[:, :-1], ((0, 0), (1, 0), (0, 0), (0, 0), (0, 0), (0, 0)))
        return jnp.concatenate([prev, t], axis=2)

    qb = blocks(q)
    kc = with_prev(blocks(k))
    vc = with_prev(blocks(v))
    scores = jnp.einsum('bnqrhe,bnkrhe->bnrhqk', qb, kc) * (E ** -0.5)
    qi = jnp.arange(DIL_BLOCK)[:, None]
    ki = jnp.arange(2 * DIL_BLOCK)[None, :]
    rel = qi + DIL_BLOCK - ki
    band = (rel >= 0) & (rel <= n_back)
    real = (jnp.arange(nb)[:, None, None] > 0) | (ki >= DIL_BLOCK)[None]
    mask = (band[None] & real)[None, :, None, None]
    bias = -slopes[:, None, None] * (dil * rel).astype(jnp.float32)
    s = jnp.where(mask, scores + bias, -jnp.inf)
    m = jnp.max(s, axis=-1, keepdims=True)
    p = jnp.exp(s - m)
    den = jnp.sum(p, axis=-1)
    den_t = jnp.transpose(den, (0, 1, 4, 2, 3))
    o = jnp.einsum('bnrhqk,bnkrhe->bnqrhe', p, vc) / den_t[..., None]
    lse = jnp.transpose(m[..., 0], (0, 1, 4, 2, 3)) + jnp.log(den_t)
    o = o.reshape(B, s_pad, H, E)[:, :S]
    lse = lse.reshape(B, s_pad, H)[:, :S]
    return o, lse


def dilated_sample(q, k_new, v_new, buf, dil, n_back, slopes):
    T, E = q.shape[1], q.shape[-1]
    Wb = buf.shape[1]
    b32 = buf.astype(jnp.float32)
    kc = jnp.concatenate([b32[:, :, 0], k_new], axis=1)
    vc = jnp.concatenate([b32[:, :, 1], v_new], axis=1)
    steps = jnp.arange(n_back + 1)
    idx = Wb + jnp.arange(T)[:, None] - dil * steps[None, :]
    valid = idx >= 0
    idxc = jnp.maximum(idx, 0)
    kg = jnp.take(kc, idxc, axis=1)
    vg = jnp.take(vc, idxc, axis=1)
    scores = jnp.einsum('bthe,btihe->bhti', q, kg) * (E ** -0.5)
    scores = scores - slopes[:, None, None] * (dil * steps).astype(jnp.float32)[None, None, :]
    s = jnp.where(valid[None, None], scores, -jnp.inf)
    m = jnp.max(s, axis=-1, keepdims=True)
    p = jnp.exp(s - m)
    den = jnp.sum(p, axis=-1)
    den_t = jnp.transpose(den, (0, 2, 1))
    o = jnp.einsum('bhti,btihe->bthe', p, vg) / den_t[..., None]
    lse = jnp.transpose(m[..., 0], (0, 2, 1)) + jnp.log(den_t)
    new_buf = jnp.stack([kc[:, T:], vc[:, T:]], axis=2).astype(buf.dtype)
    return o, lse, new_buf


def retention(q, k, v, s0):
    B, L, H, _ = q.shape
    C = L if L <= RET_CHUNK else math.gcd(L, RET_CHUNK)
    nc = L // C
    log_g = jnp.log1p(-(2.0 ** (-5.0 - jnp.arange(H, dtype=jnp.float32))))
    pos = jnp.arange(C, dtype=jnp.float32)
    rel = pos[:, None] - pos[None, :]
    dmask = jnp.where(rel >= 0, jnp.exp(log_g[:, None, None] * jnp.maximum(rel, 0.0)), 0.0)
    q_dec = jnp.exp(log_g[None, :] * (pos[:, None] + 1.0))
    k_dec = jnp.exp(log_g[None, :] * (C - 1.0 - pos[:, None]))
    chunk_dec = jnp.exp(log_g * C)

    def to_chunks(t):
        return jnp.moveaxis(t.reshape(B, nc, C, H, t.shape[-1]), 1, 0)

    def step(state, xs):
        qc, kc, vc = xs
        inner = jnp.einsum('bihd,bjhd->bhij', qc, kc) * dmask
        o = (jnp.einsum('bhij,bjhe->bihe', inner, vc)
             + jnp.einsum('bihd,bhde->bihe', qc, state) * q_dec[None, :, :, None])
        state = (state * chunk_dec[None, :, None, None]
                 + jnp.einsum('bjhd,bjhe->bhde', kc * k_dec[None, :, :, None], vc))
        return state, o

    s_fin, o = lax.scan(step, s0, (to_chunks(q), to_chunks(k), to_chunks(v)))
    o = jnp.moveaxis(o, 0, 1).reshape(B, L, H, v.shape[-1])
    return o, s_fin


def token_mixing(h, w_in, ret_gn, w_br_a, w_br_b, w_out, bufs, s0):
    B, L, _ = h.shape
    f32 = jnp.float32
    points = [int(p) for p in np.cumsum([DIL_QKV] * 3 + [RET_QK] * 2 + [RET_V] * 2 + [D_MODEL])]
    qa, ka, va, qr, kr, vr, gr, ga, gb = jnp.split(h @ w_in, points, axis=-1)

    qa = qa.reshape(B, L, N_DIL_HEADS, HEAD_DIM).astype(f32)
    ka = ka.reshape(B, L, N_DIL_HEADS, HEAD_DIM).astype(f32)
    va = va.reshape(B, L, N_DIL_HEADS, HEAD_DIM).astype(f32)
    slopes = alibi_slopes()
    outs, lses, new_bufs = [], [], []
    for g, (win, dil) in enumerate(DIL_GROUPS):
        sl = slice(g * HEADS_PER_GROUP, (g + 1) * HEADS_PER_GROUP)
        q_g, k_g, v_g = qa[:, :, sl], ka[:, :, sl], va[:, :, sl]
        if bufs is None:
            o, lse = dilated_prompt(q_g, k_g, v_g, dil, win // dil, slopes[sl])
            keep = min(win, L)
            nbuf = jnp.stack([k_g[:, L - keep:], v_g[:, L - keep:]], axis=2).astype(h.dtype)
        else:
            o, lse, nbuf = dilated_sample(q_g, k_g, v_g, bufs[g], dil, win // dil, slopes[sl])
        outs.append(o)
        lses.append(lse)
        new_bufs.append(nbuf)
    wgt = jax.nn.softmax(jnp.stack(lses), axis=0)
    o_a = jnp.sum(wgt[..., None] * jnp.stack(outs), axis=0).reshape(B, L, DIL_OUT)

    qr = qr.reshape(B, L, RET_HEADS, RET_DK).astype(f32)
    kr = kr.reshape(B, L, RET_HEADS, RET_DK).astype(f32) * (RET_DK ** -0.5)
    vr = vr.reshape(B, L, RET_HEADS, RET_DV).astype(f32)
    if s0 is None:
        st0 = jnp.zeros((B, RET_HEADS, RET_DK, RET_DV), f32)
        out_dtype = h.dtype
    else:
        st0 = s0.astype(f32)
        out_dtype = s0.dtype
    o_r, s_fin = retention(qr, kr, vr, st0)
    mu = jnp.mean(o_r, axis=-1, keepdims=True)
    var = jnp.mean(jnp.square(o_r - mu), axis=-1, keepdims=True)
    o_n = ((o_r - mu) * lax.rsqrt(var + GN_EPS)).reshape(B, L, RET_V) * ret_gn.astype(f32)
    y_r = jax.nn.silu(gr.astype(f32)) * o_n

    ya = o_a.astype(h.dtype) @ w_br_a
    yb = y_r.astype(h.dtype) @ w_br_b
    merged = jax.nn.sigmoid(ga) * ya + jax.nn.sigmoid(gb) * yb
    return merged @ w_out, new_bufs, s_fin.astype(out_dtype)


def trunk(x, bufs, s0, P):
    new_bufs = [[] for _ in DIL_GROUPS]
    new_states = []
    for l in range(DEPTH):
        x = x + 0.5 * swiglu(rmsnorm(x, P['norm_ffn1'][l]), P['ffn1_wg'][l], P['ffn1_wu'][l], P['ffn1_wd'][l])
        lb = None if bufs is None else tuple(b[l] for b in bufs)
        ls = None if s0 is None else s0[l]
        mix, nb, ns = token_mixing(rmsnorm(x, P['norm_mix'][l]), P['w_in'][l], P['ret_gn'][l],
                                   P['w_br_a'][l], P['w_br_b'][l], P['w_out'][l], lb, ls)
        x = x + mix
        x = x + 0.5 * swiglu(rmsnorm(x, P['norm_ffn2'][l]), P['ffn2_wg'][l], P['ffn2_wu'][l], P['ffn2_wd'][l])
        for g in range(len(DIL_GROUPS)):
            new_bufs[g].append(nb[g])
        new_states.append(ns)
    y = rmsnorm(x, P['norm_final'])
    return y, [jnp.stack(b) for b in new_bufs], jnp.stack(new_states)


def setup_inputs(seed: int = 0) -> dict:
    key = jax.random.key(seed)
    ks = jax.random.split(key, 24)
    f32 = jnp.float32
    nrm = lambda k, shape, s: jax.random.normal(k, shape, f32) * s
    gain = lambda k, shape: 1.0 + 0.02 * jax.random.normal(k, shape, f32)
    def buf(k, win):
        return nrm(k, (DEPTH, DEC_BATCH, min(win, PAST_LEN), 2, HEADS_PER_GROUP, HEAD_DIM), 1.0)
    return {
        'x_prompt': nrm(ks[0], (BATCH, SEQ, D_MODEL), 1.0),
        'x_sample': nrm(ks[1], (DEC_BATCH, DEC_SEQ, D_MODEL), 1.0),
        'cache_kv_w128': buf(ks[2], 128),
        'cache_kv_w512': buf(ks[3], 512),
        'cache_kv_w2048': buf(ks[4], 2048),
        'state_ret': nrm(ks[5], (DEPTH, DEC_BATCH, RET_HEADS, RET_DK, RET_DV), 0.5),
        'norm_ffn1': gain(ks[6], (DEPTH, D_MODEL)),
        'ffn1_wg': nrm(ks[7], (DEPTH, D_MODEL, D_FF), D_MODEL ** -0.5),
        'ffn1_wu': nrm(ks[8], (DEPTH, D_MODEL, D_FF), D_MODEL ** -0.5),
        'ffn1_wd': nrm(ks[9], (DEPTH, D_FF, D_MODEL), D_FF ** -0.5),
        'norm_mix': gain(ks[10], (DEPTH, D_MODEL)),
        'w_in': nrm(ks[11], (DEPTH, D_MODEL, N_IN), D_MODEL ** -0.5),
        'ret_gn': gain(ks[12], (DEPTH, RET_V)),
        'w_br_a': nrm(ks[13], (DEPTH, DIL_OUT, D_MODEL), DIL_OUT ** -0.5),
        'w_br_b': nrm(ks[14], (DEPTH, RET_V, D_MODEL), RET_V ** -0.5),
        'w_out': nrm(ks[15], (DEPTH, D_MODEL, D_MODEL), D_MODEL ** -0.5),
        'norm_ffn2': gain(ks[16], (DEPTH, D_MODEL)),
        'ffn2_wg': nrm(ks[17], (DEPTH, D_MODEL, D_FF), D_MODEL ** -0.5),
        'ffn2_wu': nrm(ks[18], (DEPTH, D_MODEL, D_FF), D_MODEL ** -0.5),
        'ffn2_wd': nrm(ks[19], (DEPTH, D_FF, D_MODEL), D_FF ** -0.5),
        'norm_final': gain(ks[20], (D_MODEL,)),
    }


def reference(x_prompt, x_sample, cache_kv_w128, cache_kv_w512, cache_kv_w2048, state_ret,
              norm_ffn1, ffn1_wg, ffn1_wu, ffn1_wd, norm_mix, w_in, ret_gn, w_br_a, w_br_b, w_out,
              norm_ffn2, ffn2_wg, ffn2_wu, ffn2_wd, norm_final):
    P = dict(norm_ffn1=norm_ffn1, ffn1_wg=ffn1_wg, ffn1_wu=ffn1_wu, ffn1_wd=ffn1_wd,
             norm_mix=norm_mix, w_in=w_in, ret_gn=ret_gn, w_br_a=w_br_a, w_br_b=w_br_b, w_out=w_out,
             norm_ffn2=norm_ffn2, ffn2_wg=ffn2_wg, ffn2_wu=ffn2_wu, ffn2_wd=ffn2_wd, norm_final=norm_final)
    y_prompt, bufs_p, ret_p = trunk(x_prompt, None, None, P)
    y_sample, bufs_s, ret_s = trunk(x_sample, (cache_kv_w128, cache_kv_w512, cache_kv_w2048), state_ret, P)
    return (y_prompt, y_sample, bufs_p[0], bufs_s[0], bufs_p[1], bufs_s[1], bufs_p[2], bufs_s[2], ret_p, ret_s)
```

```python
import functools

import numpy as np
import jax
import jax.numpy as jnp
from jax import lax
from jax.experimental import pallas as pl
from jax.experimental.pallas import tpu as pltpu

F32 = jnp.float32
BF16 = jnp.bfloat16

D_MODEL = 1024
DEPTH = 2
HEAD_DIM = 64
HEADS = 8
GROUP_W = HEADS * HEAD_DIM
DIL_GROUPS = ((128, 1), (512, 4), (2048, 16))
N_GROUPS = len(DIL_GROUPS)
N_DIL_HEADS = HEADS * N_GROUPS
QB = 128
RET_HEADS = 4
RET_DK = 128
RET_DV = 256
RET_CHUNK = 256
D_FF = 2816
FF_CHUNK = 256
NORM_EPS = 1e-6
GN_EPS = 1e-6
NEG = -1e30

PIECE_W = 512
P_Q, P_K, P_V, P_QR, P_KR, P_VR, P_GR = 0, 3, 6, 9, 10, 11, 13
N_PIECES_PROMPT = 13
N_PIECES_SAMPLE = 15
PIECE_SCALES = (HEAD_DIM ** -0.5,) * 3 + (1.0,) * 7 + (RET_DK ** -0.5,) + (1.0,) * 4

V7X_LANES = 128
LANE_CHUNKS = GROUP_W // V7X_LANES
V7X_VMEM_LIMIT_BYTES = 56 * 1024 * 1024


def _params(*sem):
    return pltpu.CompilerParams(dimension_semantics=sem, vmem_limit_bytes=V7X_VMEM_LIMIT_BYTES)


def _resident(shape):
    nd = len(shape)
    return pl.BlockSpec(shape, lambda *_: (0,) * nd, pipeline_mode=pl.Buffered(1))


def _rmsnorm(x, g):
    y = x * lax.rsqrt(jnp.mean(x * x, axis=-1, keepdims=True) + NORM_EPS)
    return y * g


def _silu(x):
    return x * jax.nn.sigmoid(x)


def _dot(a, b):
    return jnp.dot(a, b, preferred_element_type=F32)


def _dot_nt(a, b):
    return lax.dot_general(a, b, (((1,), (1,)), ((), ())), preferred_element_type=F32)


def _ffn_kernel(*refs, final_norm):
    if final_norm:
        x_ref, g_ref, wg_ref, wu_ref, wd_ref, gf_ref, o_ref, xn_ref, acc_ref = refs
    else:
        x_ref, g_ref, wg_ref, wu_ref, wd_ref, o_ref, xn_ref, acc_ref = refs
    x = x_ref[...]
    xn_ref[...] = _rmsnorm(x, g_ref[...]).astype(BF16)
    acc_ref[...] = jnp.zeros_like(acc_ref)

    def chunk(c, carry):
        xn = xn_ref[...]
        g = _dot(xn, wg_ref[c])
        u = _dot(xn, wu_ref[c])
        h = (_silu(g) * u).astype(BF16)
        acc_ref[...] += _dot(h, wd_ref[c])
        return carry

    lax.fori_loop(0, wg_ref.shape[0], chunk, 0)
    y = x + 0.5 * acc_ref[...]
    if final_norm:
        y = _rmsnorm(y, gf_ref[...])
    o_ref[...] = y


def _ffn(x, gain, wg, wu, wd, final_gain=None):
    n = x.shape[0]
    tm = min(n, 512)
    row = pl.BlockSpec((tm, D_MODEL), lambda i: (i, 0))
    in_specs = [row, _resident((1, D_MODEL)), _resident(wg.shape), _resident(wu.shape), _resident(wd.shape)]
    args = [x, gain, wg, wu, wd]
    if final_gain is not None:
        in_specs.append(_resident((1, D_MODEL)))
        args.append(final_gain)
    return pl.pallas_call(
        functools.partial(_ffn_kernel, final_norm=final_gain is not None),
        grid=(n // tm,),
        in_specs=in_specs,
        out_specs=row,
        out_shape=jax.ShapeDtypeStruct((n, D_MODEL), F32),
        scratch_shapes=[pltpu.VMEM((tm, D_MODEL), BF16), pltpu.VMEM((tm, D_MODEL), F32)],
        compiler_params=_params("parallel"),
        name="ffn",
    )(*args)


def _proj_piece(xn, w_ref, p):
    y = _dot(xn, w_ref[p])
    return y if PIECE_SCALES[p] == 1.0 else y * PIECE_SCALES[p]


def _proj_kernel(x_ref, g_ref, w_ref, o_ref):
    xn = _rmsnorm(x_ref[...], g_ref[...]).astype(BF16)
    for p in range(o_ref.shape[0]):
        o_ref[p] = _proj_piece(xn, w_ref, p).astype(o_ref.dtype)


def _proj(x, gain, w_pieces, n_pieces, out_dtype):
    n = x.shape[0]
    tm = min(n, 512)
    return pl.pallas_call(
        _proj_kernel,
        grid=(n // tm,),
        in_specs=[pl.BlockSpec((tm, D_MODEL), lambda i: (i, 0)), _resident((1, D_MODEL)),
                  pl.BlockSpec((n_pieces, D_MODEL, PIECE_W), lambda i: (0, 0, 0), pipeline_mode=pl.Buffered(1))],
        out_specs=pl.BlockSpec((n_pieces, tm, PIECE_W), lambda i: (0, i, 0)),
        out_shape=jax.ShapeDtypeStruct((n_pieces, n, PIECE_W), out_dtype),
        compiler_params=_params("parallel"),
        name="proj",
    )(x, gain, w_pieces)


NAT_PIECES = (P_Q, P_K, P_V, P_QR, P_KR, P_VR, P_VR + 1)
N_Q, N_K, N_V, N_QR, N_KR, N_VR = 0, 1, 2, 3, 4, 5


def _proj_prompt_kernel(x_ref, g_ref, w_ref, nat_ref, *rest):
    dil_refs, scr = rest[:-1], rest[-1]
    tm = x_ref.shape[0]
    xn = _rmsnorm(x_ref[...], g_ref[...]).astype(BF16)
    for i, p in enumerate(NAT_PIECES):
        nat_ref[i] = _proj_piece(xn, w_ref, p).astype(nat_ref.dtype)
    for g in range(1, N_GROUPS):
        dil = DIL_GROUPS[g][1]
        for i, p in enumerate((P_Q + g, P_K + g, P_V + g)):
            y = _proj_piece(xn, w_ref, p)
            for c in range(LANE_CHUNKS):
                scr[c] = y[:, c * V7X_LANES:(c + 1) * V7X_LANES]
            for r in range(dil):
                for c in range(LANE_CHUNKS):
                    rows = scr[c, pl.ds(r, tm // dil, stride=dil), :]
                    dil_refs[g - 1][i, r, :, c * V7X_LANES:(c + 1) * V7X_LANES] = rows.astype(BF16)


def _proj_prompt(x, gain, w_pieces, batch, seq):
    tm = min(seq, 512)
    dils = [d for _, d in DIL_GROUPS[1:]]
    out_specs = [pl.BlockSpec((len(NAT_PIECES), None, None, tm, PIECE_W), lambda b, i: (0, b, 0, i, 0))]
    out_shape = [jax.ShapeDtypeStruct((len(NAT_PIECES), batch, 1, seq, PIECE_W), BF16)]
    for d in dils:
        out_specs.append(pl.BlockSpec((3, None, d, tm // d, PIECE_W), lambda b, i: (0, b, 0, i, 0)))
        out_shape.append(jax.ShapeDtypeStruct((3, batch, d, seq // d, PIECE_W), BF16))
    return pl.pallas_call(
        _proj_prompt_kernel,
        grid=(batch, seq // tm),
        in_specs=[pl.BlockSpec((None, tm, D_MODEL), lambda b, i: (b, i, 0)), _resident((1, D_MODEL)),
                  pl.BlockSpec((N_PIECES_PROMPT, D_MODEL, PIECE_W), lambda b, i: (0, 0, 0),
                               pipeline_mode=pl.Buffered(1))],
        out_specs=out_specs,
        out_shape=out_shape,
        scratch_shapes=[pltpu.VMEM((LANE_CHUNKS, tm, V7X_LANES), F32)],
        compiler_params=_params("parallel", "parallel"),
        name="proj_prompt",
    )(x.reshape(batch, seq, D_MODEL), gain, w_pieces)


def _alibi_slopes():
    return 2.0 ** (-8.0 * np.arange(1, N_DIL_HEADS + 1, dtype=np.float64) / N_DIL_HEADS)


def _band_bias(group):
    _, dil = DIL_GROUPS[group]
    slopes = _alibi_slopes()[group * HEADS:(group + 1) * HEADS]
    qi = np.arange(QB)[:, None]
    ki = np.arange(2 * QB)[None, :]
    rel = qi + QB - ki
    valid = (rel >= 0) & (rel <= QB)
    out = np.empty((HEADS // 2, 2 * QB, 2 * QB), np.float32)
    for h in range(HEADS):
        b = np.where(valid, -slopes[h] * dil * rel, NEG)
        out[h // 2, (h % 2) * QB:(h % 2 + 1) * QB] = b
    return jnp.asarray(out)


def _attn_kernel(q_ref, kc_ref, kh_ref, vc_ref, vh_ref, bias_ref, o_ref, lse_ref, kbuf, vbuf, *, ts):
    kbuf[0:QB] = kh_ref[...]
    kbuf[QB:] = kc_ref[...]
    vbuf[0:QB] = vh_ref[...]
    vbuf[QB:] = vc_ref[...]
    first_neg = jnp.where(pl.program_id(2) == 0, NEG, 0.0).astype(F32)
    halo_cols = lax.broadcasted_iota(jnp.int32, (1, 2 * QB), 1) < QB
    first_row = jnp.where(halo_cols, first_neg, 0.0)
    low = lax.broadcasted_iota(jnp.int32, (QB, V7X_LANES), 1) < HEAD_DIM
    for i in range(ts // QB):
        rows = slice(i * QB, (i + 1) * QB)
        krows = slice(i * QB, (i + 2) * QB)
        for p in range(HEADS // 2):
            cols = slice(p * V7X_LANES, (p + 1) * V7X_LANES)
            q2 = q_ref[rows, cols]
            zero = jnp.zeros_like(q2)
            qs = jnp.concatenate([jnp.where(low, q2, zero), jnp.where(low, zero, q2)], axis=0)
            s = _dot_nt(qs, kbuf[krows, cols]) + bias_ref[p]
            if i == 0:
                s = s + first_row
            m = jnp.max(s, axis=-1, keepdims=True)
            e = jnp.exp(s - m)
            den = jnp.sum(e, axis=-1, keepdims=True)
            pv = _dot(e.astype(BF16), vbuf[krows, cols])
            on = pv / den
            ls = m + jnp.log(den)
            o_ref[rows, cols] = jnp.where(low, on[:QB], on[QB:])
            lse_ref[rows, cols] = jnp.where(low, ls[:QB], ls[QB:])


def _dilated_attention(qkv, pieces, group):
    _, batch, dil, ls, _ = qkv.shape
    ts = min(ls, 256)

    def cur(piece):
        return pl.BlockSpec((None, None, None, ts, GROUP_W), lambda b, r, n: (piece, b, r, n, 0))

    def halo(piece):
        return pl.BlockSpec((None, None, None, QB, GROUP_W),
                            lambda b, r, n: (piece, b, r, jnp.maximum(n * (ts // QB) - 1, 0), 0))

    iq, ik, iv = pieces
    out_spec = pl.BlockSpec((None, None, ts, GROUP_W), lambda b, r, n: (b, r, n, 0))
    out_sds = jax.ShapeDtypeStruct((batch, dil, ls, GROUP_W), F32)
    return pl.pallas_call(
        functools.partial(_attn_kernel, ts=ts),
        grid=(batch, dil, ls // ts),
        in_specs=[cur(iq), cur(ik), halo(ik), cur(iv), halo(iv), _resident((HEADS // 2, 2 * QB, 2 * QB))],
        out_specs=[out_spec, out_spec],
        out_shape=[out_sds, out_sds],
        scratch_shapes=[pltpu.VMEM((ts + QB, GROUP_W), BF16), pltpu.VMEM((ts + QB, GROUP_W), BF16)],
        compiler_params=_params("parallel", "parallel", "arbitrary"),
        name=f"dilated_attn_g{group}",
    )(qkv, qkv, qkv, qkv, qkv, _band_bias(group))


def _ret_log_gamma():
    return np.log1p(-(2.0 ** (-5.0 - np.arange(RET_HEADS, dtype=np.float64))))


def _ret_tables(c):
    lg = _ret_log_gamma()
    pos = np.arange(c, dtype=np.float64)
    rel = pos[:, None] - pos[None, :]
    dmask = np.where(rel >= 0, np.exp(lg[:, None, None] * np.maximum(rel, 0.0)), 0.0)
    q_dec = np.exp(lg[:, None] * (pos[None, :] + 1.0))[..., None]
    k_dec = np.exp(lg[:, None] * (c - 1.0 - pos[None, :]))[..., None]
    chunk_dec = tuple(float(v) for v in np.exp(lg * c))
    return (jnp.asarray(dmask, F32), jnp.asarray(q_dec, F32), jnp.asarray(k_dec, F32), chunk_dec)


def _group_norm_gate(o, gr, gn):
    mu = jnp.mean(o, axis=-1, keepdims=True)
    dlt = o - mu
    var = jnp.mean(dlt * dlt, axis=-1, keepdims=True)
    return _silu(gr) * ((dlt * lax.rsqrt(var + GN_EPS)) * gn)


def _ret_kernel(x_ref, q_ref, k_ref, v0_ref, v1_ref, g_ref, wgr_ref, gn_ref, dm_ref, qd_ref, kd_ref,
                y_ref, so_ref, st, *, chunk_dec):
    c = pl.program_id(1)

    @pl.when(c == 0)
    def _():
        st[...] = jnp.zeros_like(st)

    xn = _rmsnorm(x_ref[...], g_ref[...]).astype(BF16)
    gr = _dot(xn, wgr_ref[...])
    for h in range(RET_HEADS):
        q = q_ref[:, h * RET_DK:(h + 1) * RET_DK]
        k = k_ref[:, h * RET_DK:(h + 1) * RET_DK]
        v_ref = v0_ref if h < 2 else v1_ref
        v = v_ref[:, (h % 2) * RET_DV:(h % 2 + 1) * RET_DV]
        s_old = st[h]
        inner = (_dot_nt(q, k) * dm_ref[h]).astype(BF16)
        o = _dot(inner, v) + _dot(q, s_old.astype(BF16)) * qd_ref[h]
        kd = (k.astype(F32) * kd_ref[h]).astype(BF16)
        kv = lax.dot_general(kd, v, (((0,), (0,)), ((), ())), preferred_element_type=F32)
        st[h] = s_old * chunk_dec[h] + kv
        cols = slice(h * RET_DV, (h + 1) * RET_DV)
        y_ref[:, cols] = _group_norm_gate(o, gr[:, cols], gn_ref[:, cols]).astype(y_ref.dtype)

    @pl.when(c == pl.num_programs(1) - 1)
    def _():
        so_ref[...] = st[...]


def _retention(x, nat, gain, w_gr, gn):
    batch, seq, _ = x.shape
    c = min(seq, RET_CHUNK)
    dmask, q_dec, k_dec, chunk_dec = _ret_tables(c)

    def piece(p):
        return pl.BlockSpec((None, None, None, c, PIECE_W), lambda b, i: (p, b, 0, i, 0))

    return pl.pallas_call(
        functools.partial(_ret_kernel, chunk_dec=chunk_dec),
        grid=(batch, seq // c),
        in_specs=[pl.BlockSpec((None, c, D_MODEL), lambda b, i: (b, i, 0)),
                  piece(N_QR), piece(N_KR), piece(N_VR), piece(N_VR + 1),
                  _resident((1, D_MODEL)), _resident(w_gr.shape), _resident((1, D_MODEL)),
                  _resident(dmask.shape), _resident(q_dec.shape), _resident(k_dec.shape)],
        out_specs=[pl.BlockSpec((None, c, D_MODEL), lambda b, i: (b, i, 0)),
                   pl.BlockSpec((None, RET_HEADS, RET_DK, RET_DV), lambda b, i: (b, 0, 0, 0))],
        out_shape=[jax.ShapeDtypeStruct((batch, seq, D_MODEL), BF16),
                   jax.ShapeDtypeStruct((batch, RET_HEADS, RET_DK, RET_DV), F32)],
        scratch_shapes=[pltpu.VMEM((RET_HEADS, RET_DK, RET_DV), F32)],
        compiler_params=_params("parallel", "arbitrary"),
        name="retention",
    )(x, nat, nat, nat, nat, gain, w_gr, gn, dmask, q_dec, k_dec)


def _natural_rows(ref, scr):
    dil, per = ref.shape[0], ref.shape[1]
    if dil == 1:
        return ref[0]
    for r in range(dil):
        for c in range(LANE_CHUNKS):
            scr[c, pl.ds(r, per, stride=dil), :] = ref[r, :, c * V7X_LANES:(c + 1) * V7X_LANES]
    return jnp.concatenate([scr[c] for c in range(LANE_CHUNKS)], axis=1)


def _merge_kernel(x_ref, g_ref, wgab_ref, wa_ref, wb_ref, wo_ref, *rest):
    o_refs, l_refs = rest[:N_GROUPS], rest[N_GROUPS:2 * N_GROUPS]
    yr_ref, out_ref = rest[2 * N_GROUPS:2 * N_GROUPS + 2]
    scrs = list(rest[2 * N_GROUPS + 2:])
    x = x_ref[...]
    xn = _rmsnorm(x, g_ref[...]).astype(BF16)
    gates = _dot(xn, wgab_ref[...])
    lses, outs = [], []
    for g in range(N_GROUPS):
        dilated = o_refs[g].shape[0] > 1
        lses.append(_natural_rows(l_refs[g], scrs.pop(0) if dilated else None))
        outs.append(_natural_rows(o_refs[g], scrs.pop(0) if dilated else None))
    m = functools.reduce(jnp.maximum, lses)
    es = [jnp.exp(l - m) for l in lses]
    oa = sum(e * o for e, o in zip(es, outs)) / sum(es)
    ya = _dot(oa.astype(BF16), wa_ref[...])
    yb = _dot(yr_ref[...].astype(BF16), wb_ref[...])
    merged = jax.nn.sigmoid(gates[:, :D_MODEL]) * ya + jax.nn.sigmoid(gates[:, D_MODEL:]) * yb
    out_ref[...] = x + _dot(merged.astype(BF16), wo_ref[...])


def _merge(x, gain, w_gab, w_a, w_b, w_o, outs, lses, y_r):
    batch, seq, _ = x.shape
    tm = min(seq, 256)
    row = pl.BlockSpec((None, tm, D_MODEL), lambda b, i: (b, i, 0))
    dils = [o.shape[1] for o in outs]
    grp = [pl.BlockSpec((None, d, tm // d, GROUP_W), lambda b, i: (b, 0, i, 0)) for d in dils]
    return pl.pallas_call(
        _merge_kernel,
        grid=(batch, seq // tm),
        in_specs=[row, _resident((1, D_MODEL)), _resident(w_gab.shape), _resident(w_a.shape),
                  _resident(w_b.shape), _resident(w_o.shape)] + grp + grp + [row],
        out_specs=row,
        out_shape=jax.ShapeDtypeStruct((batch, seq, D_MODEL), F32),
        scratch_shapes=[pltpu.VMEM((LANE_CHUNKS, tm, V7X_LANES), F32) for d in dils if d > 1 for _ in range(2)],
        compiler_params=_params("parallel", "parallel"),
        name="merge",
    )(x, gain, w_gab, w_a, w_b, w_o, *outs, *lses, y_r)


def _kv_tail_kernel(*refs):
    x_refs, (g_ref, w_ref, o_ref) = refs[:DEPTH], refs[DEPTH:]
    layer = pl.program_id(0)
    for l in range(DEPTH):
        @pl.when(layer == l)
        def _(l=l):
            xn = _rmsnorm(x_refs[l][...], g_ref[...]).astype(BF16)
            kt = _dot_nt(w_ref[...], xn)
            o_ref[...] = kt.reshape(o_ref.shape)


def _kv_tail(xs, gains, w_kv_t, group, batch, seq):
    keep = min(DIL_GROUPS[group][0], seq)
    tr = min(keep, 512)
    nt = keep // tr
    off = (seq - keep) // tr

    def x_spec(l):
        park_b, park_i = (batch - 1, off + nt - 1) if l == 0 else (0, off)
        return pl.BlockSpec((None, tr, D_MODEL),
                            lambda d, b, i: (jnp.where(d == l, b, park_b), jnp.where(d == l, off + i, park_i), 0))

    out = pl.pallas_call(
        _kv_tail_kernel,
        grid=(DEPTH, batch, nt),
        in_specs=[x_spec(l) for l in range(DEPTH)]
        + [pl.BlockSpec((None, 1, D_MODEL), lambda d, b, i: (d, 0, 0)),
           pl.BlockSpec((None, 2 * GROUP_W, D_MODEL), lambda d, b, i: (d, 0, 0))],
        out_specs=pl.BlockSpec((None, None, 2, HEADS, HEAD_DIM, tr), lambda d, b, i: (d, b, 0, 0, 0, i)),
        out_shape=jax.ShapeDtypeStruct((DEPTH, batch, 2, HEADS, HEAD_DIM, keep), F32),
        compiler_params=_params("arbitrary", "arbitrary", "arbitrary"),
        name=f"kv_tail_g{group}",
    )(*xs, gains, w_kv_t)
    return jnp.transpose(out, (0, 1, 5, 2, 3, 4))


def _cache_bias(group, width):
    _, dil = DIL_GROUPS[group]
    slopes = _alibi_slopes()[group * HEADS:(group + 1) * HEADS]
    w = np.arange(width)
    dist = width - w
    valid = (dist % dil == 0) & (dist <= QB * dil)
    return jnp.asarray(np.where(valid[None, :], -slopes[:, None] * dist[None, :], NEG), F32)


def _head_mask():
    lane_head = lax.broadcasted_iota(jnp.int32, (HEADS, GROUP_W), 1) // HEAD_DIM
    return lane_head == lax.broadcasted_iota(jnp.int32, (HEADS, GROUP_W), 0)


def _per_head_to_row(x8, mask):
    return jnp.sum(jnp.where(mask, x8, 0.0), axis=0, keepdims=True)


def _shift_in(c_ref, o_ref, new_row):
    width = c_ref.shape[-1]
    colb = jnp.transpose(jnp.broadcast_to(new_row, (V7X_LANES, GROUP_W)))
    colb = colb.reshape(HEADS, HEAD_DIM, V7X_LANES)
    y = pltpu.roll(c_ref[...], width - 1, 2)
    last_lane = lax.broadcasted_iota(jnp.int32, colb.shape, 2) == V7X_LANES - 1
    if width > V7X_LANES:
        o_ref[:, :, :width - V7X_LANES] = y[:, :, :width - V7X_LANES]
    o_ref[:, :, width - V7X_LANES:] = jnp.where(last_lane, colb, y[:, :, width - V7X_LANES:])


def _cache_attn_kernel(*refs):
    pr_ref = refs[0]
    c_refs = refs[1:1 + N_GROUPS]
    bias_refs = refs[1 + N_GROUPS:1 + 2 * N_GROUPS]
    o_ref, lse_ref = refs[1 + 2 * N_GROUPS:3 + 2 * N_GROUPS]
    n_refs = refs[3 + 2 * N_GROUPS:3 + 3 * N_GROUPS]
    p_refs = refs[3 + 3 * N_GROUPS:3 + 4 * N_GROUPS]
    m_ref, den_ref, enew_ref = refs[3 + 4 * N_GROUPS:]
    b = pl.program_id(0)
    kv = pl.program_id(1)
    mask = _head_mask()

    def row(piece):
        return pr_ref[piece, pl.ds(b, 1), :]

    @pl.when(kv == 0)
    def _():
        for g in range(N_GROUPS):
            q, k_new = row(P_Q + g), row(P_K + g)
            kt = c_refs[g][...].reshape(GROUP_W, c_refs[g].shape[-1])
            q8 = jnp.where(mask, jnp.broadcast_to(q, (HEADS, GROUP_W)), 0.0)
            s = _dot(q8.astype(BF16), kt.astype(BF16)) + bias_refs[g][...]
            s_new = jnp.sum(q8 * k_new, axis=-1, keepdims=True)
            m = jnp.maximum(jnp.max(s, axis=-1, keepdims=True), s_new)
            e = jnp.exp(s - m)
            e_new = jnp.exp(s_new - m)
            p_refs[g][...] = e
            m_ref[g] = m
            enew_ref[g] = e_new
            den_ref[g] = jnp.sum(e, axis=-1, keepdims=True) + e_new
            _shift_in(c_refs[g], n_refs[g], k_new)

    @pl.when(kv == 1)
    def _():
        for g in range(N_GROUPS):
            v_new = row(P_V + g)
            vt = c_refs[g][...].reshape(GROUP_W, c_refs[g].shape[-1])
            o8 = _dot_nt(p_refs[g][...].astype(BF16), vt.astype(BF16)) + enew_ref[g] * v_new
            den = _per_head_to_row(den_ref[g], mask)
            o_ref[g] = _per_head_to_row(o8, mask) / den
            lse_ref[g] = _per_head_to_row(m_ref[g], mask) + jnp.log(den)
            _shift_in(c_refs[g], n_refs[g], v_new)


def _cache_attention(prs, caches_t, prev_new, layer):
    bd = prs.shape[1]
    widths = [c.shape[-1] for c in caches_t]
    cache_specs = [pl.BlockSpec((None, None, None, HEADS, HEAD_DIM, w), lambda b, kv: (layer, b, kv, 0, 0, 0))
                   for w in widths]
    small = pl.BlockSpec((N_GROUPS, None, 1, GROUP_W), lambda b, kv: (0, b, 0, 0))
    small_sds = jax.ShapeDtypeStruct((N_GROUPS, bd, 1, GROUP_W), F32)
    in_specs = [_resident(prs.shape)] + cache_specs + [_resident((HEADS, w)) for w in widths]
    args = [prs] + list(caches_t) + [_cache_bias(g, w) for g, w in enumerate(widths)]
    aliases = {}
    if prev_new is not None:
        for g in range(N_GROUPS):
            aliases[len(args)] = 2 + g
            in_specs.append(pl.BlockSpec(memory_space=pl.ANY))
            args.append(prev_new[g])

    def kernel(*refs):
        n_in = len(args)
        _cache_attn_kernel(*refs[:1 + 2 * N_GROUPS], *refs[n_in:])

    res = pl.pallas_call(
        kernel,
        grid=(bd, 2),
        in_specs=in_specs,
        out_specs=[small, small] + cache_specs,
        out_shape=[small_sds, small_sds] + [jax.ShapeDtypeStruct(c.shape, F32) for c in caches_t],
        input_output_aliases=aliases,
        scratch_shapes=[pltpu.VMEM((HEADS, w), F32) for w in widths]
        + [pltpu.VMEM((N_GROUPS, HEADS, 1), F32)] * 3,
        compiler_params=_params("arbitrary", "arbitrary"),
        name=f"cache_attn_l{layer}",
    )(*args)
    return res[0], res[1], list(res[2:])


def _ret_step_kernel(pr_ref, s_ref, gn_ref, y_ref, so_ref, *, gammas):
    b = pl.program_id(0)

    def row(piece):
        return pr_ref[piece, pl.ds(b, 1), :]

    q_all, k_all = row(P_QR), row(P_KR)
    for h in range(RET_HEADS):
        q = q_all[:, h * RET_DK:(h + 1) * RET_DK]
        k = k_all[:, h * RET_DK:(h + 1) * RET_DK]
        half = slice((h % 2) * RET_DV, (h % 2 + 1) * RET_DV)
        v = row(P_VR + h // 2)[:, half]
        gr = row(P_GR + h // 2)[:, half]
        s_old = s_ref[h]
        qs = _dot(jnp.broadcast_to(q, (8, RET_DK)).astype(BF16), s_old.astype(BF16))[0:1]
        o = jnp.sum(q * k, axis=-1, keepdims=True) * v + gammas[h] * qs
        k_col = jnp.transpose(jnp.broadcast_to(k, (V7X_LANES, RET_DK)))
        so_ref[h] = gammas[h] * s_old + jnp.concatenate([k_col, k_col], axis=1) * v
        cols = slice(h * RET_DV, (h + 1) * RET_DV)
        y_ref[:, cols] = _group_norm_gate(o, gr, gn_ref[:, cols])


def _retention_step(prs, state, gn, layer):
    bd = prs.shape[1]
    gammas = tuple(float(v) for v in np.exp(_ret_log_gamma()))
    st_shape = (RET_HEADS, RET_DK, RET_DV)
    return pl.pallas_call(
        functools.partial(_ret_step_kernel, gammas=gammas),
        grid=(bd,),
        in_specs=[_resident(prs.shape),
                  pl.BlockSpec((None, None) + st_shape, lambda b: (layer, b, 0, 0, 0)),
                  _resident((1, D_MODEL))],
        out_specs=[pl.BlockSpec((None, 1, D_MODEL), lambda b: (b, 0, 0)),
                   pl.BlockSpec((None,) + st_shape, lambda b: (b, 0, 0, 0))],
        out_shape=[jax.ShapeDtypeStruct((bd, 1, D_MODEL), F32),
                   jax.ShapeDtypeStruct((bd,) + st_shape, F32)],
        compiler_params=_params("parallel"),
        name=f"retention_step_l{layer}",
    )(prs, state, gn)


def _prep_layer(l, P):
    def ff_cols(w):
        return w.astype(BF16).reshape(D_MODEL, D_FF // FF_CHUNK, FF_CHUNK).transpose(1, 0, 2)

    def ff_rows(w):
        return w.astype(BF16).reshape(D_FF // FF_CHUNK, FF_CHUNK, D_MODEL)

    w_in = P["w_in"][l].astype(BF16)
    n_proj = N_PIECES_SAMPLE * PIECE_W
    k0 = P_K * PIECE_W
    v0 = P_V * PIECE_W
    kv_t = [jnp.concatenate([w_in[:, k0 + g * GROUP_W:k0 + (g + 1) * GROUP_W],
                             w_in[:, v0 + g * GROUP_W:v0 + (g + 1) * GROUP_W]], axis=1).T
            for g in range(N_GROUPS)]
    return dict(
        norm_ffn1=P["norm_ffn1"][l][None], norm_mix=P["norm_mix"][l][None], norm_ffn2=P["norm_ffn2"][l][None],
        ret_gn=P["ret_gn"][l][None],
        ffn1=(ff_cols(P["ffn1_wg"][l]), ff_cols(P["ffn1_wu"][l]), ff_rows(P["ffn1_wd"][l])),
        ffn2=(ff_cols(P["ffn2_wg"][l]), ff_cols(P["ffn2_wu"][l]), ff_rows(P["ffn2_wd"][l])),
        w_pieces=w_in[:, :n_proj].reshape(D_MODEL, N_PIECES_SAMPLE, PIECE_W).transpose(1, 0, 2),
        w_gr=w_in[:, P_GR * PIECE_W:n_proj],
        w_gab=w_in[:, n_proj:],
        w_kv_t=kv_t,
        w_a=P["w_br_a"][l].astype(BF16), w_b=P["w_br_b"][l].astype(BF16), w_o=P["w_out"][l].astype(BF16),
    )


def kernel(x_prompt, x_sample, cache_kv_w128, cache_kv_w512, cache_kv_w2048, state_ret,
           norm_ffn1, ffn1_wg, ffn1_wu, ffn1_wd, norm_mix, w_in, ret_gn, w_br_a, w_br_b, w_out,
           norm_ffn2, ffn2_wg, ffn2_wu, ffn2_wd, norm_final):
    P = dict(norm_ffn1=norm_ffn1, ffn1_wg=ffn1_wg, ffn1_wu=ffn1_wu, ffn1_wd=ffn1_wd, norm_mix=norm_mix,
             w_in=w_in, ret_gn=ret_gn, w_br_a=w_br_a, w_br_b=w_br_b, w_out=w_out,
             norm_ffn2=norm_ffn2, ffn2_wg=ffn2_wg, ffn2_wu=ffn2_wu, ffn2_wd=ffn2_wd)
    W = [_prep_layer(l, P) for l in range(DEPTH)]
    g_final = norm_final[None]
    batch, seq, _ = x_prompt.shape
    bd = x_sample.shape[0]

    x = x_prompt
    mix_in, ret_states = [], []
    for l, w in enumerate(W):
        x = _ffn(x.reshape(batch * seq, D_MODEL), w["norm_ffn1"], *w["ffn1"]).reshape(batch, seq, D_MODEL)
        mix_in.append(x)
        nat, *dilated = _proj_prompt(x, w["norm_mix"], w["w_pieces"], batch, seq)
        attn = [_dilated_attention(nat, (N_Q, N_K, N_V), 0)]
        attn += [_dilated_attention(dilated[g - 1], (0, 1, 2), g) for g in range(1, N_GROUPS)]
        y_r, st = _retention(x, nat, w["norm_mix"], w["w_gr"], w["ret_gn"])
        ret_states.append(st)
        x = _merge(x, w["norm_mix"], w["w_gab"], w["w_a"], w["w_b"], w["w_o"],
                   [a[0] for a in attn], [a[1] for a in attn], y_r)
        x = _ffn(x.reshape(batch * seq, D_MODEL), w["norm_ffn2"], *w["ffn2"],
                 final_gain=g_final if l == DEPTH - 1 else None).reshape(batch, seq, D_MODEL)
    y_prompt = x
    gains_mix = jnp.stack([w["norm_mix"] for w in W])
    kv_prompt = [_kv_tail(mix_in, gains_mix, jnp.stack([w["w_kv_t"][g] for w in W]), g, batch, seq)
                 for g in range(N_GROUPS)]
    ret_prompt = jnp.stack(ret_states)

    caches_t = [jnp.transpose(c, (0, 1, 3, 4, 5, 2)) for c in (cache_kv_w128, cache_kv_w512, cache_kv_w2048)]
    xs = x_sample.reshape(bd, D_MODEL)
    new_caches, ret_states = None, []
    for l, w in enumerate(W):
        xs = _ffn(xs, w["norm_ffn1"], *w["ffn1"])
        prs = _proj(xs, w["norm_mix"], w["w_pieces"], N_PIECES_SAMPLE, F32)
        o_s, lse_s, new_caches = _cache_attention(prs, caches_t, new_caches, l)
        y_r, st = _retention_step(prs, state_ret, w["ret_gn"], l)
        ret_states.append(st)
        xs = _merge(xs[None], w["norm_mix"], w["w_gab"], w["w_a"], w["w_b"], w["w_o"],
                    [o_s[g].reshape(1, 1, bd, GROUP_W) for g in range(N_GROUPS)],
                    [lse_s[g].reshape(1, 1, bd, GROUP_W) for g in range(N_GROUPS)],
                    y_r.reshape(1, bd, D_MODEL))[0]
        xs = _ffn(xs, w["norm_ffn2"], *w["ffn2"], final_gain=g_final if l == DEPTH - 1 else None)
    y_sample = xs.reshape(bd, 1, D_MODEL)
    kv_sample = [jnp.transpose(c, (0, 1, 5, 2, 3, 4)) for c in new_caches]
    ret_sample = jnp.stack(ret_states)

    return (y_prompt, y_sample, kv_prompt[0], kv_sample[0], kv_prompt[1], kv_sample[1],
            kv_prompt[2], kv_sample[2], ret_prompt, ret_sample)
```

```python
import functools

import numpy as np
import jax
import jax.numpy as jnp
from jax import lax
from jax.experimental import pallas as pl
from jax.experimental.pallas import tpu as pltpu

F32 = jnp.float32
BF16 = jnp.bfloat16

D_MODEL = 1024
DEPTH = 2
HEAD_DIM = 64
HEADS = 8
GROUP_W = HEADS * HEAD_DIM
DIL_GROUPS = ((128, 1), (512, 4), (2048, 16))
N_GROUPS = len(DIL_GROUPS)
N_DIL_HEADS = HEADS * N_GROUPS
QB = 128
RET_HEADS = 4
RET_DK = 128
RET_DV = 256
RET_CHUNK = 256
D_FF = 2816
FF_CHUNK = 256
FFN_ROWS = 512
ATTN_ROWS = 1024
MERGE_SUB_ROWS = 256
NORM_EPS = 1e-6
GN_EPS = 1e-6
NEG = -1e30

PIECE_W = 512
P_Q, P_K, P_V, P_QR, P_KR, P_VR, P_GR = 0, 3, 6, 9, 10, 11, 13
N_PIECES_PROMPT = 13
N_PIECES_SAMPLE = 15
LOG2E = float(np.log2(np.e))
LN2 = float(np.log(2.0))
SAMPLE_SCALES = (HEAD_DIM ** -0.5,) * 3 + (1.0,) * 7 + (RET_DK ** -0.5,) + (1.0,) * 4
PROMPT_SCALES = (HEAD_DIM ** -0.5 * LOG2E,) * 3 + SAMPLE_SCALES[3:]

V7X_LANES = 128
LANE_CHUNKS = GROUP_W // V7X_LANES
SUBLANE_STRIDE = 4
V7X_VMEM_LIMIT_BYTES = 56 * 1024 * 1024


def _params(*sem):
    return pltpu.CompilerParams(dimension_semantics=sem, vmem_limit_bytes=V7X_VMEM_LIMIT_BYTES)


def _resident(shape):
    nd = len(shape)
    return pl.BlockSpec(shape, lambda *_: (0,) * nd, pipeline_mode=pl.Buffered(1))


def _rmsnorm(x, g):
    y = x * lax.rsqrt(jnp.mean(x * x, axis=-1, keepdims=True) + NORM_EPS)
    return y * g


def _sigmoid(x):
    return 0.5 * jnp.tanh(0.5 * x) + 0.5


def _silu(x):
    return x * _sigmoid(x)


def _dot(a, b):
    return jnp.dot(a, b, preferred_element_type=F32)


def _dot_nt(a, b):
    return lax.dot_general(a, b, (((1,), (1,)), ((), ())), preferred_element_type=F32)


def _ffn_kernel(*refs, final_norm):
    if final_norm:
        x_ref, g_ref, wg_ref, wu_ref, wd_ref, gf_ref, o_ref, h_ref = refs
    else:
        x_ref, g_ref, wg_ref, wu_ref, wd_ref, o_ref, h_ref = refs
    x = x_ref[...]
    xn = _rmsnorm(x, g_ref[...]).astype(BF16)
    for c in range(D_FF // FF_CHUNK):
        cols = slice(c * FF_CHUNK, (c + 1) * FF_CHUNK)
        g = _dot(xn, wg_ref[:, cols])
        u = _dot(xn, wu_ref[:, cols])
        h_ref[:, cols] = (_silu(g) * u).astype(BF16)
    y = x + 0.5 * _dot(h_ref[...], wd_ref[...])
    if final_norm:
        y = _rmsnorm(y, gf_ref[...])
    o_ref[...] = y


def _ffn(x, gain, wg, wu, wd, final_gain=None):
    n = x.shape[0]
    tm = min(n, FFN_ROWS)
    row = pl.BlockSpec((tm, D_MODEL), lambda i: (i, 0))
    in_specs = [row, _resident((1, D_MODEL)), _resident(wg.shape), _resident(wu.shape), _resident(wd.shape)]
    args = [x, gain, wg, wu, wd]
    if final_gain is not None:
        in_specs.append(_resident((1, D_MODEL)))
        args.append(final_gain)
    return pl.pallas_call(
        functools.partial(_ffn_kernel, final_norm=final_gain is not None),
        grid=(n // tm,),
        in_specs=in_specs,
        out_specs=row,
        out_shape=jax.ShapeDtypeStruct((n, D_MODEL), F32),
        scratch_shapes=[pltpu.VMEM((tm, D_FF), BF16)],
        compiler_params=_params("parallel"),
        name="ffn",
    )(*args)


def _proj_piece(xn, w_ref, p, scales):
    y = _dot(xn, w_ref[:, p * PIECE_W:(p + 1) * PIECE_W])
    return y if scales[p] == 1.0 else y * scales[p]


def _proj_kernel(x_ref, g_ref, w_ref, o_ref):
    xn = _rmsnorm(x_ref[...], g_ref[...]).astype(BF16)
    for p in range(o_ref.shape[0]):
        o_ref[p] = _proj_piece(xn, w_ref, p, SAMPLE_SCALES).astype(o_ref.dtype)


def _proj(x, gain, w_pieces, n_pieces, out_dtype):
    n = x.shape[0]
    tm = min(n, FFN_ROWS)
    return pl.pallas_call(
        _proj_kernel,
        grid=(n // tm,),
        in_specs=[pl.BlockSpec((tm, D_MODEL), lambda i: (i, 0)), _resident((1, D_MODEL)),
                  pl.BlockSpec((D_MODEL, n_pieces * PIECE_W), lambda i: (0, 0), pipeline_mode=pl.Buffered(1))],
        out_specs=pl.BlockSpec((n_pieces, tm, PIECE_W), lambda i: (0, i, 0)),
        out_shape=jax.ShapeDtypeStruct((n_pieces, n, PIECE_W), out_dtype),
        compiler_params=_params("parallel"),
        name="proj",
    )(x, gain, w_pieces)


NAT_PIECES = (P_Q, P_K, P_V, P_QR, P_KR, P_VR, P_VR + 1)
N_Q, N_K, N_V, N_QR, N_KR, N_VR = 0, 1, 2, 3, 4, 5


def _kv_window(group, seq, tm):
    keep = min(DIL_GROUPS[group][0], seq)
    return keep, min(keep, tm), (seq - keep) // tm


def _proj_prompt_kernel(x_ref, g_ref, w_ref, nat_ref, *rest):
    dil_refs = rest[:N_GROUPS - 1]
    kv_refs = rest[N_GROUPS - 1:2 * N_GROUPS - 1]
    xn_ref, scr = rest[-2:]
    tm = x_ref.shape[0]
    xn_ref[...] = _rmsnorm(x_ref[...], g_ref[...]).astype(BF16)

    def piece(p):
        return _proj_piece(xn_ref[...], w_ref, p, PROMPT_SCALES)

    def keep_window(g, which, y):
        tw = kv_refs[g].shape[-1]
        kv_refs[g][which] = jnp.transpose(y[tm - tw:]).reshape(HEADS, HEAD_DIM, tw)

    def natural(i):
        p = NAT_PIECES[i]
        y = piece(p)
        if p in (P_K, P_V):
            keep_window(0, (P_K, P_V).index(p), y)
        nat_ref[i] = y.astype(nat_ref.dtype)

    def dilated(n):
        g, i = 1 + n // 3, n % 3
        dil = DIL_GROUPS[g][1]
        y = piece((P_Q, P_K, P_V)[i] + g)
        if i > 0:
            keep_window(g, i - 1, y)
        buf, buf2 = scr.at[n % 2, 0], scr.at[n % 2, 1]
        for c in range(LANE_CHUNKS):
            buf[c] = y[:, c * V7X_LANES:(c + 1) * V7X_LANES]
        outer = dil // SUBLANE_STRIDE if dil > SUBLANE_STRIDE else 1
        inner = dil // outer
        if outer > 1:
            per = tm // inner
            for r0 in range(inner):
                for c in range(LANE_CHUNKS):
                    buf2[c, r0 * per:(r0 + 1) * per, :] = buf[c, pl.ds(r0, per, stride=inner), :]
            buf = buf2
        for r0 in range(inner):
            for r1 in range(outer):
                start = r0 * (tm // inner) + r1 if outer > 1 else r0
                for c in range(LANE_CHUNKS):
                    rows = buf[c, pl.ds(start, tm // dil, stride=outer if outer > 1 else inner), :]
                    dil_refs[g - 1][i, r1 * inner + r0, :, c * V7X_LANES:(c + 1) * V7X_LANES] = rows.astype(BF16)

    n_dilated = 3 * (N_GROUPS - 1)
    for n in range(max(n_dilated, len(NAT_PIECES))):
        if n < len(NAT_PIECES):
            natural(n)
        if n < n_dilated:
            dilated(n)


def _proj_prompt(x, gain, w_pieces, layer, prev_kv):
    batch, seq, _ = x.shape
    tm = min(seq, FFN_ROWS)
    out_specs = [pl.BlockSpec((len(NAT_PIECES), None, None, tm, PIECE_W), lambda b, i: (0, b, 0, i, 0))]
    out_shape = [jax.ShapeDtypeStruct((len(NAT_PIECES), batch, 1, seq, PIECE_W), BF16)]
    for _, d in DIL_GROUPS[1:]:
        out_specs.append(pl.BlockSpec((3, None, d, tm // d, PIECE_W), lambda b, i: (0, b, 0, i, 0)))
        out_shape.append(jax.ShapeDtypeStruct((3, batch, d, seq // d, PIECE_W), BF16))
    for g in range(N_GROUPS):
        keep, tw, first = _kv_window(g, seq, tm)
        out_specs.append(pl.BlockSpec((None, None, 2, HEADS, HEAD_DIM, tw),
                                      lambda b, i, first=first: (layer, b, 0, 0, 0, jnp.maximum(i - first, 0))))
        out_shape.append(jax.ShapeDtypeStruct((DEPTH, batch, 2, HEADS, HEAD_DIM, keep), F32))
    in_specs = [pl.BlockSpec((None, tm, D_MODEL), lambda b, i: (b, i, 0)), _resident((1, D_MODEL)),
                pl.BlockSpec((D_MODEL, N_PIECES_PROMPT * PIECE_W), lambda b, i: (0, 0),
                             pipeline_mode=pl.Buffered(1))]
    args = [x, gain, w_pieces]
    aliases = {}
    if prev_kv is not None:
        for g in range(N_GROUPS):
            aliases[len(args)] = N_GROUPS + g
            in_specs.append(pl.BlockSpec(memory_space=pl.ANY))
            args.append(prev_kv[g])
    n_in = len(args)

    def kernel(*refs):
        _proj_prompt_kernel(*refs[:3], *refs[n_in:])

    res = pl.pallas_call(
        kernel,
        grid=(batch, seq // tm),
        in_specs=in_specs,
        out_specs=out_specs,
        out_shape=out_shape,
        input_output_aliases=aliases,
        scratch_shapes=[pltpu.VMEM((tm, D_MODEL), BF16), pltpu.VMEM((2, 2, LANE_CHUNKS, tm, V7X_LANES), F32)],
        compiler_params=_params("arbitrary", "arbitrary"),
        name=f"proj_prompt_l{layer}",
    )(*args)
    return res[0], list(res[1:N_GROUPS]), list(res[N_GROUPS:])


def _alibi_slopes():
    return 2.0 ** (-8.0 * np.arange(1, N_DIL_HEADS + 1, dtype=np.float64) / N_DIL_HEADS)


def _band_bias(group):
    _, dil = DIL_GROUPS[group]
    slopes = _alibi_slopes()[group * HEADS:(group + 1) * HEADS]
    qi = np.arange(QB)[:, None]
    ki = np.arange(2 * QB)[None, :]
    rel = qi + QB - ki
    valid = (rel >= 0) & (rel <= QB)
    out = np.empty((HEADS // 2, 2 * QB, 2 * QB), np.float32)
    for h in range(HEADS):
        b = np.where(valid, -slopes[h] * dil * rel * LOG2E, NEG)
        out[h // 2, (h % 2) * QB:(h % 2 + 1) * QB] = b
    return jnp.asarray(out)


def _attn_kernel(q_ref, kc_ref, kh_ref, vc_ref, vh_ref, bias_ref, o_ref, lse_ref):
    nr, ts, _ = q_ref.shape
    first_neg = jnp.where(pl.program_id(2) == 0, NEG, 0.0).astype(F32)
    halo_cols = lax.broadcasted_iota(jnp.int32, (1, 2 * QB), 1) < QB
    first_row = jnp.where(halo_cols, first_neg, 0.0)
    low = lax.broadcasted_iota(jnp.int32, (QB, V7X_LANES), 1) < HEAD_DIM

    def per_head(x):
        return jnp.where(low, x[:QB], x[QB:])

    for r in range(nr):
        for i in range(ts // QB):
            rows = slice(i * QB, (i + 1) * QB)
            for p in range(HEADS // 2):
                cols = slice(p * V7X_LANES, (p + 1) * V7X_LANES)
                q2 = q_ref[r, rows, cols]
                zero = jnp.zeros_like(q2)
                qs = jnp.concatenate([jnp.where(low, q2, zero), jnp.where(low, zero, q2)], axis=0)
                if i == 0:
                    k2 = jnp.concatenate([kh_ref[r, :, cols], kc_ref[r, 0:QB, cols]], axis=0)
                    v2 = jnp.concatenate([vh_ref[r, :, cols], vc_ref[r, 0:QB, cols]], axis=0)
                else:
                    k2 = kc_ref[r, (i - 1) * QB:(i + 1) * QB, cols]
                    v2 = vc_ref[r, (i - 1) * QB:(i + 1) * QB, cols]
                s = _dot_nt(qs, k2) + bias_ref[p]
                if i == 0:
                    s = s + first_row
                m = jnp.max(s, axis=-1, keepdims=True)
                e = jnp.exp2(s - m)
                den = per_head(jnp.sum(e, axis=-1, keepdims=True))
                pv = per_head(_dot(e.astype(BF16), v2))
                o_ref[r, rows, cols] = pv / den
                lse_ref[r, rows, cols] = (per_head(m) + jnp.log2(den)) * LN2


def _dilated_attention(qkv, pieces, group):
    _, batch, dil, ls, _ = qkv.shape
    ts = min(ls, ATTN_ROWS)
    nr = min(dil, ATTN_ROWS // ts)

    def cur(piece):
        return pl.BlockSpec((None, None, nr, ts, GROUP_W), lambda b, r, n: (piece, b, r, n, 0))

    def halo(piece):
        return pl.BlockSpec((None, None, nr, QB, GROUP_W),
                            lambda b, r, n: (piece, b, r, jnp.maximum(n * (ts // QB) - 1, 0), 0))

    iq, ik, iv = pieces
    out_spec = pl.BlockSpec((None, nr, ts, GROUP_W), lambda b, r, n: (b, r, n, 0))
    out_sds = jax.ShapeDtypeStruct((batch, dil, ls, GROUP_W), F32)
    return pl.pallas_call(
        _attn_kernel,
        grid=(batch, dil // nr, ls // ts),
        in_specs=[cur(iq), cur(ik), halo(ik), cur(iv), halo(iv), _resident((HEADS // 2, 2 * QB, 2 * QB))],
        out_specs=[out_spec, out_spec],
        out_shape=[out_sds, out_sds],
        compiler_params=_params("parallel", "parallel", "arbitrary"),
        name=f"dilated_attn_g{group}",
    )(qkv, qkv, qkv, qkv, qkv, _band_bias(group))


def _ret_log_gamma():
    return np.log1p(-(2.0 ** (-5.0 - np.arange(RET_HEADS, dtype=np.float64))))


def _ret_tables(c):
    lg = _ret_log_gamma()
    pos = np.arange(c, dtype=np.float64)
    rel = pos[:, None] - pos[None, :]
    dmask = np.where(rel >= 0, np.exp(lg[:, None, None] * np.maximum(rel, 0.0)), 0.0)
    q_dec = np.exp(lg[:, None] * (pos[None, :] + 1.0))[..., None]
    k_dec = np.exp(lg[:, None] * (c - 1.0 - pos[None, :]))[..., None]
    chunk_dec = tuple(float(v) for v in np.exp(lg * c))
    return (jnp.asarray(dmask, F32), jnp.asarray(q_dec, F32), jnp.asarray(k_dec, F32), chunk_dec)


def _group_norm_gate(o, gr, gn):
    mu = jnp.mean(o, axis=-1, keepdims=True)
    dlt = o - mu
    var = jnp.mean(dlt * dlt, axis=-1, keepdims=True)
    return _silu(gr) * ((dlt * lax.rsqrt(var + GN_EPS)) * gn)


def _ret_kernel(x_ref, q_ref, k_ref, v0_ref, v1_ref, g_ref, wgr_ref, gn_ref, dm_ref, qd_ref, kd_ref,
                y_ref, so_ref, st, *, chunk_dec):
    c = pl.program_id(1)

    @pl.when(c == 0)
    def _():
        st[...] = jnp.zeros_like(st)

    chunk = dm_ref.shape[1]
    for j in range(x_ref.shape[0] // chunk):
        rows = slice(j * chunk, (j + 1) * chunk)
        xn = _rmsnorm(x_ref[rows, :], g_ref[...]).astype(BF16)
        gr = _dot(xn, wgr_ref[...])
        for h in range(RET_HEADS):
            q = q_ref[rows, h * RET_DK:(h + 1) * RET_DK]
            k = k_ref[rows, h * RET_DK:(h + 1) * RET_DK]
            v_ref = v0_ref if h < 2 else v1_ref
            v = v_ref[rows, (h % 2) * RET_DV:(h % 2 + 1) * RET_DV]
            s_old = st[h]
            inner = (_dot_nt(q, k) * dm_ref[h]).astype(BF16)
            o = _dot(inner, v) + _dot(q, s_old.astype(BF16)) * qd_ref[h]
            kd = (k.astype(F32) * kd_ref[h]).astype(BF16)
            kv = lax.dot_general(kd, v, (((0,), (0,)), ((), ())), preferred_element_type=F32)
            st[h] = s_old * chunk_dec[h] + kv
            cols = slice(h * RET_DV, (h + 1) * RET_DV)
            y_ref[rows, cols] = _group_norm_gate(o, gr[:, cols], gn_ref[:, cols]).astype(y_ref.dtype)

    @pl.when(c == pl.num_programs(1) - 1)
    def _():
        so_ref[...] = st[...]


def _retention(x, nat, gain, w_gr, gn):
    batch, seq, _ = x.shape
    dmask, q_dec, k_dec, chunk_dec = _ret_tables(min(seq, RET_CHUNK))
    c = min(seq, FFN_ROWS)

    def piece(p):
        return pl.BlockSpec((None, None, None, c, PIECE_W), lambda b, i: (p, b, 0, i, 0))

    return pl.pallas_call(
        functools.partial(_ret_kernel, chunk_dec=chunk_dec),
        grid=(batch, seq // c),
        in_specs=[pl.BlockSpec((None, c, D_MODEL), lambda b, i: (b, i, 0)),
                  piece(N_QR), piece(N_KR), piece(N_VR), piece(N_VR + 1),
                  _resident((1, D_MODEL)), _resident(w_gr.shape), _resident((1, D_MODEL)),
                  _resident(dmask.shape), _resident(q_dec.shape), _resident(k_dec.shape)],
        out_specs=[pl.BlockSpec((None, c, D_MODEL), lambda b, i: (b, i, 0)),
                   pl.BlockSpec((None, RET_HEADS, RET_DK, RET_DV), lambda b, i: (b, 0, 0, 0))],
        out_shape=[jax.ShapeDtypeStruct((batch, seq, D_MODEL), BF16),
                   jax.ShapeDtypeStruct((batch, RET_HEADS, RET_DK, RET_DV), F32)],
        scratch_shapes=[pltpu.VMEM((RET_HEADS, RET_DK, RET_DV), F32)],
        compiler_params=_params("parallel", "arbitrary"),
        name="retention",
    )(x, nat, nat, nat, nat, gain, w_gr, gn, dmask, q_dec, k_dec)


def _natural_rows(ref, scr):
    dil, per = ref.shape[0], ref.shape[1]
    if dil == 1:
        return ref[0]
    outer = dil // SUBLANE_STRIDE if dil > SUBLANE_STRIDE else 1
    inner = dil // outer
    buf, buf2 = scr.at[0], scr.at[1]
    first = buf2 if outer > 1 else buf
    for r0 in range(inner):
        for r1 in range(outer):
            start = r0 * per * outer + r1 if outer > 1 else r0
            for c in range(LANE_CHUNKS):
                first[c, pl.ds(start, per, stride=outer if outer > 1 else inner), :] = (
                    ref[r1 * inner + r0, :, c * V7X_LANES:(c + 1) * V7X_LANES])
    if outer > 1:
        for r0 in range(inner):
            for c in range(LANE_CHUNKS):
                buf[c, pl.ds(r0, per * outer, stride=inner), :] = buf2[c, r0 * per * outer:(r0 + 1) * per * outer, :]
    return jnp.concatenate([buf[c] for c in range(LANE_CHUNKS)], axis=1)


def _merge_kernel(x_ref, g_ref, wgab_ref, wa_ref, wb_ref, wo_ref, *rest):
    o_refs, l_refs = rest[:N_GROUPS], rest[N_GROUPS:2 * N_GROUPS]
    yr_ref, out_ref = rest[2 * N_GROUPS:2 * N_GROUPS + 2]
    scrs = list(rest[2 * N_GROUPS + 2:])
    lses, outs = [], []
    for g in range(N_GROUPS):
        dilated = o_refs[g].shape[0] > 1
        lses.append(_natural_rows(l_refs[g], scrs.pop(0) if dilated else None))
        outs.append(_natural_rows(o_refs[g], scrs.pop(0) if dilated else None))
    tm = x_ref.shape[0]
    sub = min(tm, MERGE_SUB_ROWS)
    for h in range(tm // sub):
        rows = slice(h * sub, (h + 1) * sub)
        x = x_ref[rows, :]
        yb = _dot(yr_ref[rows, :].astype(BF16), wb_ref[...])
        xn = _rmsnorm(x, g_ref[...]).astype(BF16)
        gates = _dot(xn, wgab_ref[...])
        ls = [l[rows] for l in lses]
        m = functools.reduce(jnp.maximum, ls)
        es = [jnp.exp(l - m) for l in ls]
        oa = sum(e * o[rows] for e, o in zip(es, outs)) / sum(es)
        ya = _dot(oa.astype(BF16), wa_ref[...])
        merged = _sigmoid(gates[:, :D_MODEL]) * ya + _sigmoid(gates[:, D_MODEL:]) * yb
        out_ref[rows, :] = x + _dot(merged.astype(BF16), wo_ref[...])


def _merge(x, gain, w_gab, w_a, w_b, w_o, outs, lses, y_r):
    batch, seq, _ = x.shape
    tm = min(seq, FFN_ROWS)
    row = pl.BlockSpec((None, tm, D_MODEL), lambda b, i: (b, i, 0))
    dils = [o.shape[1] for o in outs]
    grp = [pl.BlockSpec((None, d, tm // d, GROUP_W), lambda b, i: (b, 0, i, 0)) for d in dils]
    return pl.pallas_call(
        _merge_kernel,
        grid=(batch, seq // tm),
        in_specs=[row, _resident((1, D_MODEL)), _resident(w_gab.shape), _resident(w_a.shape),
                  _resident(w_b.shape), _resident(w_o.shape)] + grp + grp + [row],
        out_specs=row,
        out_shape=jax.ShapeDtypeStruct((batch, seq, D_MODEL), F32),
        scratch_shapes=[pltpu.VMEM((2, LANE_CHUNKS, tm, V7X_LANES), F32) for d in dils if d > 1 for _ in range(2)],
        compiler_params=_params("parallel", "parallel"),
        name="merge",
    )(x, gain, w_gab, w_a, w_b, w_o, *outs, *lses, y_r)


def _cache_bias(group, width):
    _, dil = DIL_GROUPS[group]
    slopes = _alibi_slopes()[group * HEADS:(group + 1) * HEADS]
    w = np.arange(width)
    dist = width - w
    valid = (dist % dil == 0) & (dist <= QB * dil)
    return jnp.asarray(np.where(valid[None, :], -slopes[:, None] * dist[None, :], NEG), F32)


def _head_mask():
    lane_head = lax.broadcasted_iota(jnp.int32, (HEADS, GROUP_W), 1) // HEAD_DIM
    return lane_head == lax.broadcasted_iota(jnp.int32, (HEADS, GROUP_W), 0)


def _per_head_to_row(x8, mask):
    return jnp.sum(jnp.where(mask, x8, 0.0), axis=0, keepdims=True)


def _shift_in(c_ref, o_ref, new_row):
    width = c_ref.shape[-1]
    colb = jnp.transpose(jnp.broadcast_to(new_row, (V7X_LANES, GROUP_W)))
    colb = colb.reshape(HEADS, HEAD_DIM, V7X_LANES)
    y = pltpu.roll(c_ref[...], width - 1, 2)
    last_lane = lax.broadcasted_iota(jnp.int32, colb.shape, 2) == V7X_LANES - 1
    if width > V7X_LANES:
        o_ref[:, :, :width - V7X_LANES] = y[:, :, :width - V7X_LANES]
    o_ref[:, :, width - V7X_LANES:] = jnp.where(last_lane, colb, y[:, :, width - V7X_LANES:])


def _cache_attn_kernel(*refs):
    pr_ref = refs[0]
    c_refs = refs[1:1 + N_GROUPS]
    bias_refs = refs[1 + N_GROUPS:1 + 2 * N_GROUPS]
    o_ref, lse_ref = refs[1 + 2 * N_GROUPS:3 + 2 * N_GROUPS]
    n_refs = refs[3 + 2 * N_GROUPS:3 + 3 * N_GROUPS]
    p_refs = refs[3 + 3 * N_GROUPS:3 + 4 * N_GROUPS]
    m_ref, den_ref, enew_ref = refs[3 + 4 * N_GROUPS:]
    b = pl.program_id(0)
    kv = pl.program_id(1)
    mask = _head_mask()

    def row(piece):
        return pr_ref[piece, pl.ds(b, 1), :]

    @pl.when(kv == 0)
    def _():
        for g in range(N_GROUPS):
            q, k_new = row(P_Q + g), row(P_K + g)
            kt = c_refs[g][...].reshape(GROUP_W, c_refs[g].shape[-1])
            q8 = jnp.where(mask, jnp.broadcast_to(q, (HEADS, GROUP_W)), 0.0)
            s = _dot(q8.astype(BF16), kt.astype(BF16)) + bias_refs[g][...]
            s_new = jnp.sum(q8 * k_new, axis=-1, keepdims=True)
            m = jnp.maximum(jnp.max(s, axis=-1, keepdims=True), s_new)
            e = jnp.exp(s - m)
            e_new = jnp.exp(s_new - m)
            p_refs[g][...] = e
            m_ref[g] = m
            enew_ref[g] = e_new
            den_ref[g] = jnp.sum(e, axis=-1, keepdims=True) + e_new
            _shift_in(c_refs[g], n_refs[g], k_new)

    @pl.when(kv == 1)
    def _():
        for g in range(N_GROUPS):
            v_new = row(P_V + g)
            vt = c_refs[g][...].reshape(GROUP_W, c_refs[g].shape[-1])
            o8 = _dot_nt(p_refs[g][...].astype(BF16), vt.astype(BF16)) + enew_ref[g] * v_new
            den = _per_head_to_row(den_ref[g], mask)
            o_ref[g] = _per_head_to_row(o8, mask) / den
            lse_ref[g] = _per_head_to_row(m_ref[g], mask) + jnp.log(den)
            _shift_in(c_refs[g], n_refs[g], v_new)


def _cache_attention(prs, caches_t, prev_new, layer):
    bd = prs.shape[1]
    widths = [c.shape[-1] for c in caches_t]
    cache_specs = [pl.BlockSpec((None, None, None, HEADS, HEAD_DIM, w), lambda b, kv: (layer, b, kv, 0, 0, 0))
                   for w in widths]
    small = pl.BlockSpec((N_GROUPS, None, 1, GROUP_W), lambda b, kv: (0, b, 0, 0))
    small_sds = jax.ShapeDtypeStruct((N_GROUPS, bd, 1, GROUP_W), F32)
    in_specs = [_resident(prs.shape)] + cache_specs + [_resident((HEADS, w)) for w in widths]
    args = [prs] + list(caches_t) + [_cache_bias(g, w) for g, w in enumerate(widths)]
    aliases = {}
    if prev_new is not None:
        for g in range(N_GROUPS):
            aliases[len(args)] = 2 + g
            in_specs.append(pl.BlockSpec(memory_space=pl.ANY))
            args.append(prev_new[g])

    def kernel(*refs):
        n_in = len(args)
        _cache_attn_kernel(*refs[:1 + 2 * N_GROUPS], *refs[n_in:])

    res = pl.pallas_call(
        kernel,
        grid=(bd, 2),
        in_specs=in_specs,
        out_specs=[small, small] + cache_specs,
        out_shape=[small_sds, small_sds] + [jax.ShapeDtypeStruct(c.shape, F32) for c in caches_t],
        input_output_aliases=aliases,
        scratch_shapes=[pltpu.VMEM((HEADS, w), F32) for w in widths]
        + [pltpu.VMEM((N_GROUPS, HEADS, 1), F32)] * 3,
        compiler_params=_params("arbitrary", "arbitrary"),
        name=f"cache_attn_l{layer}",
    )(*args)
    return res[0], res[1], list(res[2:])


def _ret_step_kernel(pr_ref, s_ref, gn_ref, y_ref, so_ref, *, gammas):
    b = pl.program_id(0)

    def row(piece):
        return pr_ref[piece, pl.ds(b, 1), :]

    q_all, k_all = row(P_QR), row(P_KR)
    for h in range(RET_HEADS):
        q = q_all[:, h * RET_DK:(h + 1) * RET_DK]
        k = k_all[:, h * RET_DK:(h + 1) * RET_DK]
        half = slice((h % 2) * RET_DV, (h % 2 + 1) * RET_DV)
        v = row(P_VR + h // 2)[:, half]
        gr = row(P_GR + h // 2)[:, half]
        s_old = s_ref[h]
        qs = _dot(jnp.broadcast_to(q, (8, RET_DK)).astype(BF16), s_old.astype(BF16))[0:1]
        o = jnp.sum(q * k, axis=-1, keepdims=True) * v + gammas[h] * qs
        k_col = jnp.transpose(jnp.broadcast_to(k, (V7X_LANES, RET_DK)))
        so_ref[h] = gammas[h] * s_old + jnp.concatenate([k_col, k_col], axis=1) * v
        cols = slice(h * RET_DV, (h + 1) * RET_DV)
        y_ref[:, cols] = _group_norm_gate(o, gr, gn_ref[:, cols])


def _retention_step(prs, state, gn, layer):
    bd = prs.shape[1]
    gammas = tuple(float(v) for v in np.exp(_ret_log_gamma()))
    st_shape = (RET_HEADS, RET_DK, RET_DV)
    return pl.pallas_call(
        functools.partial(_ret_step_kernel, gammas=gammas),
        grid=(bd,),
        in_specs=[_resident(prs.shape),
                  pl.BlockSpec((None, None) + st_shape, lambda b: (layer, b, 0, 0, 0)),
                  _resident((1, D_MODEL))],
        out_specs=[pl.BlockSpec((None, 1, D_MODEL), lambda b: (b, 0, 0)),
                   pl.BlockSpec((None,) + st_shape, lambda b: (b, 0, 0, 0))],
        out_shape=[jax.ShapeDtypeStruct((bd, 1, D_MODEL), F32),
                   jax.ShapeDtypeStruct((bd,) + st_shape, F32)],
        compiler_params=_params("parallel"),
        name=f"retention_step_l{layer}",
    )(prs, state, gn)


def _prep_layer(l, P):
    w_in = P["w_in"][l].astype(BF16)
    n_proj = N_PIECES_SAMPLE * PIECE_W
    return dict(
        norm_ffn1=P["norm_ffn1"][l][None], norm_mix=P["norm_mix"][l][None], norm_ffn2=P["norm_ffn2"][l][None],
        ret_gn=P["ret_gn"][l][None],
        ffn1=tuple(P[k][l].astype(BF16) for k in ("ffn1_wg", "ffn1_wu", "ffn1_wd")),
        ffn2=tuple(P[k][l].astype(BF16) for k in ("ffn2_wg", "ffn2_wu", "ffn2_wd")),
        w_pieces=w_in,
        w_gr=w_in[:, P_GR * PIECE_W:n_proj],
        w_gab=w_in[:, n_proj:],
        w_a=P["w_br_a"][l].astype(BF16), w_b=P["w_br_b"][l].astype(BF16), w_o=P["w_out"][l].astype(BF16),
    )


def kernel(x_prompt, x_sample, cache_kv_w128, cache_kv_w512, cache_kv_w2048, state_ret,
           norm_ffn1, ffn1_wg, ffn1_wu, ffn1_wd, norm_mix, w_in, ret_gn, w_br_a, w_br_b, w_out,
           norm_ffn2, ffn2_wg, ffn2_wu, ffn2_wd, norm_final):
    P = dict(norm_ffn1=norm_ffn1, ffn1_wg=ffn1_wg, ffn1_wu=ffn1_wu, ffn1_wd=ffn1_wd, norm_mix=norm_mix,
             w_in=w_in, ret_gn=ret_gn, w_br_a=w_br_a, w_br_b=w_br_b, w_out=w_out,
             norm_ffn2=norm_ffn2, ffn2_wg=ffn2_wg, ffn2_wu=ffn2_wu, ffn2_wd=ffn2_wd)
    W = [_prep_layer(l, P) for l in range(DEPTH)]
    g_final = norm_final[None]
    batch, seq, _ = x_prompt.shape
    bd = x_sample.shape[0]

    x = x_prompt
    kv_windows, ret_states = None, []
    for l, w in enumerate(W):
        x = _ffn(x.reshape(batch * seq, D_MODEL), w["norm_ffn1"], *w["ffn1"]).reshape(batch, seq, D_MODEL)
        nat, dilated, kv_windows = _proj_prompt(x, w["norm_mix"], w["w_pieces"], l, kv_windows)
        attn = [_dilated_attention(nat, (N_Q, N_K, N_V), 0)]
        attn += [_dilated_attention(dilated[g - 1], (0, 1, 2), g) for g in range(1, N_GROUPS)]
        y_r, st = _retention(x, nat, w["norm_mix"], w["w_gr"], w["ret_gn"])
        ret_states.append(st)
        x = _merge(x, w["norm_mix"], w["w_gab"], w["w_a"], w["w_b"], w["w_o"],
                   [a[0] for a in attn], [a[1] for a in attn], y_r)
        x = _ffn(x.reshape(batch * seq, D_MODEL), w["norm_ffn2"], *w["ffn2"],
                 final_gain=g_final if l == DEPTH - 1 else None).reshape(batch, seq, D_MODEL)
    y_prompt = x
    kv_prompt = [jnp.transpose(kv, (0, 1, 5, 2, 3, 4)) for kv in kv_windows]
    ret_prompt = jnp.stack(ret_states)

    caches_t = [jnp.transpose(c, (0, 1, 3, 4, 5, 2)) for c in (cache_kv_w128, cache_kv_w512, cache_kv_w2048)]
    xs = x_sample.reshape(bd, D_MODEL)
    new_caches, ret_states = None, []
    for l, w in enumerate(W):
        xs = _ffn(xs, w["norm_ffn1"], *w["ffn1"])
        prs = _proj(xs, w["norm_mix"], w["w_pieces"], N_PIECES_SAMPLE, F32)
        o_s, lse_s, new_caches = _cache_attention(prs, caches_t, new_caches, l)
        y_r, st = _retention_step(prs, state_ret, w["ret_gn"], l)
        ret_states.append(st)
        xs = _merge(xs[None], w["norm_mix"], w["w_gab"], w["w_a"], w["w_b"], w["w_o"],
                    [o_s[g].reshape(1, 1, bd, GROUP_W) for g in range(N_GROUPS)],
                    [lse_s[g].reshape(1, 1, bd, GROUP_W) for g in range(N_GROUPS)],
                    y_r.reshape(1, bd, D_MODEL))[0]
        xs = _ffn(xs, w["norm_ffn2"], *w["ffn2"], final_gain=g_final if l == DEPTH - 1 else None)
    y_sample = xs.reshape(bd, 1, D_MODEL)
    kv_sample = [jnp.transpose(c, (0, 1, 5, 2, 3, 4)) for c in new_caches]
    ret_sample = jnp.stack(ret_states)

    return (y_prompt, y_sample, kv_prompt[0], kv_sample[0], kv_prompt[1], kv_sample[1],
            kv_prompt[2], kv_sample[2], ret_prompt, ret_sample)
```

```python
import functools

import numpy as np
import jax
import jax.numpy as jnp
from jax import lax
from jax.experimental import pallas as pl
from jax.experimental.pallas import tpu as pltpu

F32 = jnp.float32
BF16 = jnp.bfloat16

D_MODEL = 1024
DEPTH = 2
HEAD_DIM = 64
HEADS = 8
GROUP_W = HEADS * HEAD_DIM
DIL_GROUPS = ((128, 1), (512, 4), (2048, 16))
N_GROUPS = len(DIL_GROUPS)
N_DIL_HEADS = HEADS * N_GROUPS
QB = 128
RET_HEADS = 4
RET_DK = 128
RET_DV = 256
RET_CHUNK = 256
D_FF = 2816
FF_CHUNK = 256
FFN_ROWS = 512
ATTN_ROWS = 1024
MERGE_SUB_ROWS = 256
NORM_EPS = 1e-6
GN_EPS = 1e-6
NEG = -1e30

PIECE_W = 512
P_Q, P_K, P_V, P_QR, P_KR, P_VR, P_GR = 0, 3, 6, 9, 10, 11, 13
N_PIECES_PROMPT = 13
N_PIECES_SAMPLE = 15
LOG2E = float(np.log2(np.e))
LN2 = float(np.log(2.0))
SAMPLE_SCALES = (HEAD_DIM ** -0.5,) * 3 + (1.0,) * 7 + (RET_DK ** -0.5,) + (1.0,) * 4
PROMPT_SCALES = (HEAD_DIM ** -0.5 * LOG2E,) * 3 + SAMPLE_SCALES[3:]

RET_STEP_SEQS = 4

V7X_LANES = 128
V7X_SUBLANES = 8
LANE_CHUNKS = GROUP_W // V7X_LANES
SUBLANE_STRIDE = 4
V7X_VMEM_LIMIT_BYTES = 56 * 1024 * 1024


def _params(*sem):
    return pltpu.CompilerParams(dimension_semantics=sem, vmem_limit_bytes=V7X_VMEM_LIMIT_BYTES)


def _resident(shape):
    nd = len(shape)
    return pl.BlockSpec(shape, lambda *_: (0,) * nd, pipeline_mode=pl.Buffered(1))


def _layer_block(w, layer, cols=None, col_block=0):
    cols = w.shape[2] if cols is None else cols
    return pl.BlockSpec((None, w.shape[1], cols), lambda *_: (layer, 0, col_block), pipeline_mode=pl.Buffered(1))


def _rmsnorm(x, g):
    y = x * lax.rsqrt(jnp.mean(x * x, axis=-1, keepdims=True) + NORM_EPS)
    return y * g


def _sigmoid(x):
    return 0.5 * jnp.tanh(0.5 * x) + 0.5


def _silu(x):
    return x * _sigmoid(x)


def _dot(a, b):
    return jnp.dot(a, b, preferred_element_type=F32)


def _dot_nt(a, b):
    return lax.dot_general(a, b, (((1,), (1,)), ((), ())), preferred_element_type=F32)


def _ffn_kernel(*refs, final_norm):
    if final_norm:
        x_ref, g_ref, wg_ref, wu_ref, wd_ref, gf_ref, o_ref, h_ref = refs
    else:
        x_ref, g_ref, wg_ref, wu_ref, wd_ref, o_ref, h_ref = refs
    x = x_ref[...]
    xn = _rmsnorm(x, g_ref[...]).astype(BF16)
    for c in range(D_FF // FF_CHUNK):
        cols = slice(c * FF_CHUNK, (c + 1) * FF_CHUNK)
        g = _dot(xn, wg_ref[:, cols])
        u = _dot(xn, wu_ref[:, cols])
        h_ref[:, cols] = (_silu(g) * u).astype(BF16)
    y = x + 0.5 * _dot(h_ref[...], wd_ref[...])
    if final_norm:
        y = _rmsnorm(y, gf_ref[...])
    o_ref[...] = y


def _ffn(x, gain, wg, wu, wd, layer, final_gain=None):
    n = x.shape[0]
    tm = min(n, FFN_ROWS)
    row = pl.BlockSpec((tm, D_MODEL), lambda i: (i, 0))
    in_specs = [row, _resident((1, D_MODEL))] + [_layer_block(w, layer) for w in (wg, wu, wd)]
    args = [x, gain, wg, wu, wd]
    if final_gain is not None:
        in_specs.append(_resident((1, D_MODEL)))
        args.append(final_gain)
    return pl.pallas_call(
        functools.partial(_ffn_kernel, final_norm=final_gain is not None),
        grid=(n // tm,),
        in_specs=in_specs,
        out_specs=row,
        out_shape=jax.ShapeDtypeStruct((n, D_MODEL), F32),
        scratch_shapes=[pltpu.VMEM((tm, D_FF), BF16)],
        compiler_params=_params("parallel"),
        name="ffn",
    )(*args)


def _proj_piece(xn, w_ref, p, scales):
    y = _dot(xn, w_ref[:, p * PIECE_W:(p + 1) * PIECE_W])
    return y if scales[p] == 1.0 else y * scales[p]


def _proj_kernel(x_ref, g_ref, w_ref, o_ref):
    xn = _rmsnorm(x_ref[...], g_ref[...]).astype(BF16)
    for p in range(o_ref.shape[0]):
        o_ref[p] = _proj_piece(xn, w_ref, p, SAMPLE_SCALES).astype(o_ref.dtype)


def _proj(x, gain, w_in, layer, n_pieces, out_dtype):
    n = x.shape[0]
    tm = min(n, FFN_ROWS)
    return pl.pallas_call(
        _proj_kernel,
        grid=(n // tm,),
        in_specs=[pl.BlockSpec((tm, D_MODEL), lambda i: (i, 0)), _resident((1, D_MODEL)),
                  _layer_block(w_in, layer, n_pieces * PIECE_W)],
        out_specs=pl.BlockSpec((n_pieces, tm, PIECE_W), lambda i: (0, i, 0)),
        out_shape=jax.ShapeDtypeStruct((n_pieces, n, PIECE_W), out_dtype),
        compiler_params=_params("parallel"),
        name="proj",
    )(x, gain, w_in)


NAT_PIECES = (P_Q, P_K, P_V, P_QR, P_KR, P_VR, P_VR + 1)
N_Q, N_K, N_V, N_QR, N_KR, N_VR = 0, 1, 2, 3, 4, 5


def _kv_window(group, seq, tm):
    keep = min(DIL_GROUPS[group][0], seq)
    return keep, min(keep, tm), (seq - keep) // tm


def _proj_prompt_kernel(x_ref, g_ref, w_ref, nat_ref, *rest):
    dil_refs = rest[:N_GROUPS - 1]
    krt_ref = rest[N_GROUPS - 1]
    kv_refs = rest[N_GROUPS:2 * N_GROUPS]
    xn_ref, scr = rest[-2:]
    tm = x_ref.shape[0]
    xn_ref[...] = _rmsnorm(x_ref[...], g_ref[...]).astype(BF16)

    def piece(p):
        return _proj_piece(xn_ref[...], w_ref, p, PROMPT_SCALES)

    def keep_window(g, which, y):
        tw = kv_refs[g].shape[-1]
        kv_refs[g][which] = jnp.transpose(y[tm - tw:]).reshape(HEADS, HEAD_DIM, tw)

    def natural(i):
        p = NAT_PIECES[i]
        y = piece(p)
        if p in (P_K, P_V):
            keep_window(0, (P_K, P_V).index(p), y)
        if p == P_KR:
            krt_ref[...] = jnp.transpose(y).astype(krt_ref.dtype)
        nat_ref[i] = y.astype(nat_ref.dtype)

    def dilated(n):
        g, i = 1 + n // 3, n % 3
        dil = DIL_GROUPS[g][1]
        y = piece((P_Q, P_K, P_V)[i] + g)
        if i > 0:
            keep_window(g, i - 1, y)
        buf, buf2 = scr.at[n % 2, 0], scr.at[n % 2, 1]
        for c in range(LANE_CHUNKS):
            buf[c] = y[:, c * V7X_LANES:(c + 1) * V7X_LANES]
        outer = dil // SUBLANE_STRIDE if dil > SUBLANE_STRIDE else 1
        inner = dil // outer
        if outer > 1:
            per = tm // inner
            for r0 in range(inner):
                for c in range(LANE_CHUNKS):
                    buf2[c, r0 * per:(r0 + 1) * per, :] = buf[c, pl.ds(r0, per, stride=inner), :]
            buf = buf2
        for r0 in range(inner):
            for r1 in range(outer):
                start = r0 * (tm // inner) + r1 if outer > 1 else r0
                for c in range(LANE_CHUNKS):
                    rows = buf[c, pl.ds(start, tm // dil, stride=outer if outer > 1 else inner), :]
                    dil_refs[g - 1][i, r1 * inner + r0, :, c * V7X_LANES:(c + 1) * V7X_LANES] = rows.astype(BF16)

    n_dilated = 3 * (N_GROUPS - 1)
    for n in range(max(n_dilated, len(NAT_PIECES))):
        if n < len(NAT_PIECES):
            natural(n)
        if n < n_dilated:
            dilated(n)


def _proj_prompt(x, gain, w_in, layer, prev_kv):
    batch, seq, _ = x.shape
    tm = min(seq, FFN_ROWS)
    out_specs = [pl.BlockSpec((len(NAT_PIECES), None, None, tm, PIECE_W), lambda b, i: (0, b, 0, i, 0))]
    out_shape = [jax.ShapeDtypeStruct((len(NAT_PIECES), batch, 1, seq, PIECE_W), BF16)]
    for _, d in DIL_GROUPS[1:]:
        out_specs.append(pl.BlockSpec((3, None, d, tm // d, PIECE_W), lambda b, i: (0, b, 0, i, 0)))
        out_shape.append(jax.ShapeDtypeStruct((3, batch, d, seq // d, PIECE_W), BF16))
    out_specs.append(pl.BlockSpec((None, PIECE_W, tm), lambda b, i: (b, 0, i)))
    out_shape.append(jax.ShapeDtypeStruct((batch, PIECE_W, seq), BF16))
    n_plain = len(out_specs)
    for g in range(N_GROUPS):
        keep, tw, first = _kv_window(g, seq, tm)
        out_specs.append(pl.BlockSpec((None, None, 2, HEADS, HEAD_DIM, tw),
                                      lambda b, i, first=first: (layer, b, 0, 0, 0, jnp.maximum(i - first, 0))))
        out_shape.append(jax.ShapeDtypeStruct((DEPTH, batch, 2, HEADS, HEAD_DIM, keep), F32))
    in_specs = [pl.BlockSpec((None, tm, D_MODEL), lambda b, i: (b, i, 0)), _resident((1, D_MODEL)),
                _layer_block(w_in, layer, N_PIECES_PROMPT * PIECE_W)]
    args = [x, gain, w_in]
    aliases = {}
    if prev_kv is not None:
        for g in range(N_GROUPS):
            aliases[len(args)] = n_plain + g
            in_specs.append(pl.BlockSpec(memory_space=pl.ANY))
            args.append(prev_kv[g])
    n_in = len(args)

    def kernel(*refs):
        _proj_prompt_kernel(*refs[:3], *refs[n_in:])

    res = pl.pallas_call(
        kernel,
        grid=(batch, seq // tm),
        in_specs=in_specs,
        out_specs=out_specs,
        out_shape=out_shape,
        input_output_aliases=aliases,
        scratch_shapes=[pltpu.VMEM((tm, D_MODEL), BF16), pltpu.VMEM((2, 2, LANE_CHUNKS, tm, V7X_LANES), F32)],
        compiler_params=_params("arbitrary", "arbitrary"),
        name=f"proj_prompt_l{layer}",
    )(*args)
    return res[0], list(res[1:N_GROUPS]), res[N_GROUPS], list(res[n_plain:])


def _alibi_slopes():
    return 2.0 ** (-8.0 * np.arange(1, N_DIL_HEADS + 1, dtype=np.float64) / N_DIL_HEADS)


def _band_bias(group):
    _, dil = DIL_GROUPS[group]
    slopes = _alibi_slopes()[group * HEADS:(group + 1) * HEADS]
    qi = np.arange(QB)[:, None]
    ki = np.arange(2 * QB)[None, :]
    rel = qi + QB - ki
    valid = (rel >= 0) & (rel <= QB)
    out = np.empty((HEADS // 2, 2 * QB, 2 * QB), np.float32)
    for h in range(HEADS):
        b = np.where(valid, -slopes[h] * dil * rel * LOG2E, NEG)
        out[h // 2, (h % 2) * QB:(h % 2 + 1) * QB] = b
    return jnp.asarray(out)


def _attn_kernel(q_ref, kc_ref, kh_ref, vc_ref, vh_ref, bias_ref, o_ref, lse_ref):
    nr, ts, _ = q_ref.shape
    first_neg = jnp.where(pl.program_id(2) == 0, NEG, 0.0).astype(F32)
    halo_cols = lax.broadcasted_iota(jnp.int32, (1, 2 * QB), 1) < QB
    first_row = jnp.where(halo_cols, first_neg, 0.0)
    low = lax.broadcasted_iota(jnp.int32, (QB, V7X_LANES), 1) < HEAD_DIM

    def per_head(x):
        return jnp.where(low, x[:QB], x[QB:])

    for r in range(nr):
        for i in range(ts // QB):
            rows = slice(i * QB, (i + 1) * QB)
            for p in range(HEADS // 2):
                cols = slice(p * V7X_LANES, (p + 1) * V7X_LANES)
                q2 = q_ref[r, rows, cols]
                zero = jnp.zeros_like(q2)
                qs = jnp.concatenate([jnp.where(low, q2, zero), jnp.where(low, zero, q2)], axis=0)
                if i == 0:
                    k2 = jnp.concatenate([kh_ref[r, :, cols], kc_ref[r, 0:QB, cols]], axis=0)
                    v2 = jnp.concatenate([vh_ref[r, :, cols], vc_ref[r, 0:QB, cols]], axis=0)
                else:
                    k2 = kc_ref[r, (i - 1) * QB:(i + 1) * QB, cols]
                    v2 = vc_ref[r, (i - 1) * QB:(i + 1) * QB, cols]
                s = _dot_nt(qs, k2) + bias_ref[p]
                if i == 0:
                    s = s + first_row
                m = jnp.max(s, axis=-1, keepdims=True)
                e = jnp.exp2(s - m)
                den = per_head(jnp.sum(e, axis=-1, keepdims=True))
                pv = per_head(_dot(e.astype(BF16), v2))
                o_ref[r, rows, cols] = pv / den
                lse_ref[r, rows, cols] = (per_head(m) + jnp.log2(den)) * LN2


def _dilated_attention(qkv, pieces, group):
    _, batch, dil, ls, _ = qkv.shape
    ts = min(ls, ATTN_ROWS)
    nr = min(dil, ATTN_ROWS // ts)

    def cur(piece):
        return pl.BlockSpec((None, None, nr, ts, GROUP_W), lambda b, r, n: (piece, b, r, n, 0))

    def halo(piece):
        return pl.BlockSpec((None, None, nr, QB, GROUP_W),
                            lambda b, r, n: (piece, b, r, jnp.maximum(n * (ts // QB) - 1, 0), 0))

    iq, ik, iv = pieces
    out_spec = pl.BlockSpec((None, nr, ts, GROUP_W), lambda b, r, n: (b, r, n, 0))
    out_sds = jax.ShapeDtypeStruct((batch, dil, ls, GROUP_W), F32)
    return pl.pallas_call(
        _attn_kernel,
        grid=(batch, dil // nr, ls // ts),
        in_specs=[cur(iq), cur(ik), halo(ik), cur(iv), halo(iv), _resident((HEADS // 2, 2 * QB, 2 * QB))],
        out_specs=[out_spec, out_spec],
        out_shape=[out_sds, out_sds],
        compiler_params=_params("parallel", "parallel", "arbitrary"),
        name=f"dilated_attn_g{group}",
    )(qkv, qkv, qkv, qkv, qkv, _band_bias(group))


def _ret_log_gamma():
    return np.log1p(-(2.0 ** (-5.0 - np.arange(RET_HEADS, dtype=np.float64))))


def _ret_tables(c):
    lg = _ret_log_gamma()
    pos = np.arange(c, dtype=np.float64)
    rel = pos[:, None] - pos[None, :]
    dmask = np.where(rel >= 0, np.exp(lg[:, None, None] * np.maximum(rel, 0.0)), 0.0)
    q_dec = np.exp(lg[:, None] * (pos[None, :] + 1.0))[..., None]
    k_dec = np.exp(lg[:, None] * (c - 1.0 - pos[None, :]))[:, None, :]
    chunk_dec = tuple(float(v) for v in np.exp(lg * c))
    return (jnp.asarray(dmask, F32), jnp.asarray(q_dec, F32), jnp.asarray(k_dec, F32), chunk_dec)


def _group_norm_gate(o, gr, gn):
    mu = jnp.mean(o, axis=-1, keepdims=True)
    dlt = o - mu
    var = jnp.mean(dlt * dlt, axis=-1, keepdims=True)
    return _silu(gr) * ((dlt * lax.rsqrt(var + GN_EPS)) * gn)


def _ret_kernel(x_ref, q_ref, k_ref, kt_ref, v0_ref, v1_ref, g_ref, wgr0_ref, wgr1_ref, gn_ref, dm_ref, qd_ref, kd_ref,
                y_ref, so_ref, st, *, chunk_dec):
    c = pl.program_id(1)

    @pl.when(c == 0)
    def _():
        st[...] = jnp.zeros_like(st)

    chunk = dm_ref.shape[1]
    for j in range(x_ref.shape[0] // chunk):
        rows = slice(j * chunk, (j + 1) * chunk)
        xn = _rmsnorm(x_ref[rows, :], g_ref[...]).astype(BF16)
        gr = (_dot(xn, wgr0_ref[...]), _dot(xn, wgr1_ref[...]))
        for h in range(RET_HEADS):
            q = q_ref[rows, h * RET_DK:(h + 1) * RET_DK]
            k = k_ref[rows, h * RET_DK:(h + 1) * RET_DK]
            v_ref = v0_ref if h < 2 else v1_ref
            half = slice((h % 2) * RET_DV, (h % 2 + 1) * RET_DV)
            v = v_ref[rows, half]
            s_old = st[h]
            inner = (_dot_nt(q, k) * dm_ref[h]).astype(BF16)
            o = _dot(inner, v) + _dot(q, s_old.astype(BF16)) * qd_ref[h]
            kdt = (kt_ref[h * RET_DK:(h + 1) * RET_DK, rows].astype(F32) * kd_ref[h]).astype(BF16)
            st[h] = s_old * chunk_dec[h] + _dot(kdt, v)
            cols = slice(h * RET_DV, (h + 1) * RET_DV)
            y_ref[rows, cols] = _group_norm_gate(o, gr[h // 2][:, half], gn_ref[:, cols]).astype(y_ref.dtype)

    @pl.when(c == pl.num_programs(1) - 1)
    def _():
        so_ref[...] = st[...]


def _retention(x, nat, kr_t, gain, w_in, layer, gn):
    batch, seq, _ = x.shape
    dmask, q_dec, k_dec, chunk_dec = _ret_tables(min(seq, RET_CHUNK))
    c = min(seq, FFN_ROWS)

    def piece(p):
        return pl.BlockSpec((None, None, None, c, PIECE_W), lambda b, i: (p, b, 0, i, 0))

    st_shape = (RET_HEADS, RET_DK, RET_DV)
    return pl.pallas_call(
        functools.partial(_ret_kernel, chunk_dec=chunk_dec),
        grid=(batch, seq // c),
        in_specs=[pl.BlockSpec((None, c, D_MODEL), lambda b, i: (b, i, 0)),
                  piece(N_QR), piece(N_KR), pl.BlockSpec((None, PIECE_W, c), lambda b, i: (b, 0, i)),
                  piece(N_VR), piece(N_VR + 1),
                  _resident((1, D_MODEL)), _layer_block(w_in, layer, PIECE_W, P_GR),
                  _layer_block(w_in, layer, PIECE_W, P_GR + 1), _resident((1, D_MODEL)),
                  _resident(dmask.shape), _resident(q_dec.shape), _resident(k_dec.shape)],
        out_specs=[pl.BlockSpec((None, c, D_MODEL), lambda b, i: (b, i, 0)),
                   pl.BlockSpec((None,) + st_shape, lambda b, i: (b, 0, 0, 0))],
        out_shape=[jax.ShapeDtypeStruct((batch, seq, D_MODEL), BF16),
                   jax.ShapeDtypeStruct((batch,) + st_shape, F32)],
        scratch_shapes=[pltpu.VMEM(st_shape, F32)],
        compiler_params=_params("parallel", "arbitrary"),
        name="retention",
    )(x, nat, nat, kr_t, nat, nat, gain, w_in, w_in, gn, dmask, q_dec, k_dec)


def _natural_rows(ref, scr):
    dil, per = ref.shape[0], ref.shape[1]
    if dil == 1:
        return ref[0]
    outer = dil // SUBLANE_STRIDE if dil > SUBLANE_STRIDE else 1
    inner = dil // outer
    buf, buf2 = scr.at[0], scr.at[1]
    first = buf2 if outer > 1 else buf
    for r0 in range(inner):
        for r1 in range(outer):
            start = r0 * per * outer + r1 if outer > 1 else r0
            for c in range(LANE_CHUNKS):
                first[c, pl.ds(start, per, stride=outer if outer > 1 else inner), :] = (
                    ref[r1 * inner + r0, :, c * V7X_LANES:(c + 1) * V7X_LANES])
    if outer > 1:
        for r0 in range(inner):
            for c in range(LANE_CHUNKS):
                buf[c, pl.ds(r0, per * outer, stride=inner), :] = buf2[c, r0 * per * outer:(r0 + 1) * per * outer, :]
    return jnp.concatenate([buf[c] for c in range(LANE_CHUNKS)], axis=1)


def _merge_kernel(x_ref, g_ref, wga0_ref, wga1_ref, wgb0_ref, wgb1_ref, wa_ref, wb_ref, wo_ref, *rest):
    o_refs, l_refs = rest[:N_GROUPS], rest[N_GROUPS:2 * N_GROUPS]
    yr_ref, out_ref = rest[2 * N_GROUPS:2 * N_GROUPS + 2]
    scrs = list(rest[2 * N_GROUPS + 2:])
    lses, outs = [], []
    for g in range(N_GROUPS):
        dilated = o_refs[g].shape[0] > 1
        lses.append(_natural_rows(l_refs[g], scrs.pop(0) if dilated else None))
        outs.append(_natural_rows(o_refs[g], scrs.pop(0) if dilated else None))
    tm = x_ref.shape[0]
    sub = min(tm, MERGE_SUB_ROWS)
    def inputs(h):
        rows = slice(h * sub, (h + 1) * sub)
        x = x_ref[rows, :]
        yb = _dot(yr_ref[rows, :].astype(BF16), wb_ref[...])
        xn = _rmsnorm(x, g_ref[...]).astype(BF16)
        ga = jnp.concatenate([_dot(xn, wga0_ref[...]), _dot(xn, wga1_ref[...])], axis=1)
        gb = jnp.concatenate([_dot(xn, wgb0_ref[...]), _dot(xn, wgb1_ref[...])], axis=1)
        ls = [l[rows] for l in lses]
        m = functools.reduce(jnp.maximum, ls)
        es = [jnp.exp(l - m) for l in ls]
        oa = sum(e * o[rows] for e, o in zip(es, outs)) / sum(es)
        ya = _dot(oa.astype(BF16), wa_ref[...])
        return rows, x, ga, gb, ya, yb

    def output(rows, x, ga, gb, ya, yb):
        merged = _sigmoid(ga) * ya + _sigmoid(gb) * yb
        out_ref[rows, :] = x + _dot(merged.astype(BF16), wo_ref[...])

    pending = None
    for h in range(tm // sub):
        staged = inputs(h)
        if pending is not None:
            output(*pending)
        pending = staged
    output(*pending)


def _merge(x, gain, w_in, w_a, w_b, w_o, layer, outs, lses, y_r):
    batch, seq, _ = x.shape
    tm = min(seq, FFN_ROWS)
    row = pl.BlockSpec((None, tm, D_MODEL), lambda b, i: (b, i, 0))
    dils = [o.shape[1] for o in outs]
    grp = [pl.BlockSpec((None, d, tm // d, GROUP_W), lambda b, i: (b, 0, i, 0)) for d in dils]
    gate_w = [_layer_block(w_in, layer, PIECE_W, N_PIECES_SAMPLE + j) for j in range(4)]
    return pl.pallas_call(
        _merge_kernel,
        grid=(batch, seq // tm),
        in_specs=[row, _resident((1, D_MODEL))] + gate_w + [_layer_block(w, layer) for w in (w_a, w_b, w_o)]
        + grp + grp + [row],
        out_specs=row,
        out_shape=jax.ShapeDtypeStruct((batch, seq, D_MODEL), F32),
        scratch_shapes=[pltpu.VMEM((2, LANE_CHUNKS, tm, V7X_LANES), F32) for d in dils if d > 1 for _ in range(2)],
        compiler_params=_params("parallel", "parallel"),
        name="merge",
    )(x, gain, w_in, w_in, w_in, w_in, w_a, w_b, w_o, *outs, *lses, y_r)


def _cache_bias(group, width):
    _, dil = DIL_GROUPS[group]
    slopes = _alibi_slopes()[group * HEADS:(group + 1) * HEADS]
    w = np.arange(width)
    dist = width - w
    valid = (dist % dil == 0) & (dist <= QB * dil)
    return jnp.asarray(np.where(valid[None, :], -slopes[:, None] * dist[None, :], NEG), F32)


def _head_mask():
    lane_head = lax.broadcasted_iota(jnp.int32, (HEADS, GROUP_W), 1) // HEAD_DIM
    return lane_head == lax.broadcasted_iota(jnp.int32, (HEADS, GROUP_W), 0)


def _per_head_to_row(x8, mask):
    return jnp.sum(jnp.where(mask, x8, 0.0), axis=0, keepdims=True)


def _shift_in(c_ref, o_ref, new_row):
    width = c_ref.shape[-1]
    colb = jnp.transpose(jnp.broadcast_to(new_row, (V7X_LANES, GROUP_W)))
    colb = colb.reshape(HEADS, HEAD_DIM, V7X_LANES)
    y = pltpu.roll(c_ref[...], width - 1, 2)
    last_lane = lax.broadcasted_iota(jnp.int32, colb.shape, 2) == V7X_LANES - 1
    if width > V7X_LANES:
        o_ref[:, :, :width - V7X_LANES] = y[:, :, :width - V7X_LANES]
    o_ref[:, :, width - V7X_LANES:] = jnp.where(last_lane, colb, y[:, :, width - V7X_LANES:])


def _cache_attn_kernel(*refs):
    pr_ref = refs[0]
    c_refs = refs[1:1 + N_GROUPS]
    bias_refs = refs[1 + N_GROUPS:1 + 2 * N_GROUPS]
    o_ref, lse_ref = refs[1 + 2 * N_GROUPS:3 + 2 * N_GROUPS]
    n_refs = refs[3 + 2 * N_GROUPS:3 + 3 * N_GROUPS]
    p_refs = refs[3 + 3 * N_GROUPS:3 + 4 * N_GROUPS]
    m_ref, den_ref, enew_ref = refs[3 + 4 * N_GROUPS:]
    b = pl.program_id(0)
    kv = pl.program_id(1)
    mask = _head_mask()

    def row(piece):
        return pr_ref[piece, pl.ds(b, 1), :]

    @pl.when(kv == 0)
    def _():
        for g in range(N_GROUPS):
            q, k_new = row(P_Q + g), row(P_K + g)
            kt = c_refs[g][...].reshape(GROUP_W, c_refs[g].shape[-1])
            q8 = jnp.where(mask, jnp.broadcast_to(q, (HEADS, GROUP_W)), 0.0)
            s = _dot(q8.astype(BF16), kt.astype(BF16)) + bias_refs[g][...]
            s_new = jnp.sum(q8 * k_new, axis=-1, keepdims=True)
            m = jnp.maximum(jnp.max(s, axis=-1, keepdims=True), s_new)
            e = jnp.exp(s - m)
            e_new = jnp.exp(s_new - m)
            p_refs[g][...] = e
            m_ref[g] = m
            enew_ref[g] = e_new
            den_ref[g] = jnp.sum(e, axis=-1, keepdims=True) + e_new
            _shift_in(c_refs[g], n_refs[g], k_new)

    @pl.when(kv == 1)
    def _():
        for g in range(N_GROUPS):
            v_new = row(P_V + g)
            vt = c_refs[g][...].reshape(GROUP_W, c_refs[g].shape[-1])
            o8 = _dot_nt(p_refs[g][...].astype(BF16), vt.astype(BF16)) + enew_ref[g] * v_new
            den = _per_head_to_row(den_ref[g], mask)
            o_ref[g] = _per_head_to_row(o8, mask) / den
            lse_ref[g] = _per_head_to_row(m_ref[g], mask) + jnp.log(den)
            _shift_in(c_refs[g], n_refs[g], v_new)


def _cache_attention(prs, caches_t, prev_new, layer):
    bd = prs.shape[1]
    widths = [c.shape[-1] for c in caches_t]
    cache_specs = [pl.BlockSpec((None, None, None, HEADS, HEAD_DIM, w), lambda b, kv: (layer, b, kv, 0, 0, 0))
                   for w in widths]
    small = pl.BlockSpec((N_GROUPS, None, 1, GROUP_W), lambda b, kv: (0, b, 0, 0))
    small_sds = jax.ShapeDtypeStruct((N_GROUPS, bd, 1, GROUP_W), F32)
    in_specs = [_resident(prs.shape)] + cache_specs + [_resident((HEADS, w)) for w in widths]
    args = [prs] + list(caches_t) + [_cache_bias(g, w) for g, w in enumerate(widths)]
    aliases = {}
    if prev_new is not None:
        for g in range(N_GROUPS):
            aliases[len(args)] = 2 + g
            in_specs.append(pl.BlockSpec(memory_space=pl.ANY))
            args.append(prev_new[g])

    def kernel(*refs):
        n_in = len(args)
        _cache_attn_kernel(*refs[:1 + 2 * N_GROUPS], *refs[n_in:])

    res = pl.pallas_call(
        kernel,
        grid=(bd, 2),
        in_specs=in_specs,
        out_specs=[small, small] + cache_specs,
        out_shape=[small_sds, small_sds] + [jax.ShapeDtypeStruct(c.shape, F32) for c in caches_t],
        input_output_aliases=aliases,
        scratch_shapes=[pltpu.VMEM((HEADS, w), F32) for w in widths]
        + [pltpu.VMEM((N_GROUPS, HEADS, 1), F32)] * 3,
        compiler_params=_params("arbitrary", "arbitrary"),
        name=f"cache_attn_l{layer}",
    )(*args)
    return res[0], res[1], list(res[2:])


def _ret_step_kernel(pr_ref, s_ref, gn_ref, *rest, gammas):
    y_ref, so_ref = rest[-2:]
    n_seq = s_ref.shape[0]
    first = pl.program_id(0) * n_seq
    for j in range(n_seq):
        def row(piece):
            return pr_ref[piece, pl.ds(first + j, 1), :]

        q_all, k_all = row(P_QR), row(P_KR)
        for h in range(RET_HEADS):
            q = q_all[:, h * RET_DK:(h + 1) * RET_DK]
            k = k_all[:, h * RET_DK:(h + 1) * RET_DK]
            half = slice((h % 2) * RET_DV, (h % 2 + 1) * RET_DV)
            v = row(P_VR + h // 2)[:, half]
            gr = row(P_GR + h // 2)[:, half]
            s_old = s_ref[j, h]
            qs = _dot(jnp.broadcast_to(q, (V7X_SUBLANES, RET_DK)).astype(BF16), s_old.astype(BF16))[0:1]
            o = jnp.sum(q * k, axis=-1, keepdims=True) * v + gammas[h] * qs
            k_col = jnp.transpose(jnp.broadcast_to(k, (V7X_LANES, RET_DK)))
            so_ref[j, h] = gammas[h] * s_old + jnp.concatenate([k_col, k_col], axis=1) * v
            cols = slice(h * RET_DV, (h + 1) * RET_DV)
            y_ref[j, :, cols] = _group_norm_gate(o, gr, gn_ref[:, cols])


def _retention_step(prs, state, gn, layer, prev_new):
    bd = prs.shape[1]
    ns = RET_STEP_SEQS if bd % RET_STEP_SEQS == 0 else 1
    gammas = tuple(float(v) for v in np.exp(_ret_log_gamma()))
    st_block = pl.BlockSpec((None, ns, RET_HEADS, RET_DK, RET_DV), lambda b: (layer, b, 0, 0, 0))
    in_specs = [_resident(prs.shape), st_block, _resident((1, D_MODEL))]
    args = [prs, state, gn]
    aliases = {}
    if prev_new is not None:
        aliases[len(args)] = 1
        in_specs.append(pl.BlockSpec(memory_space=pl.ANY))
        args.append(prev_new)
    return pl.pallas_call(
        functools.partial(_ret_step_kernel, gammas=gammas),
        grid=(bd // ns,),
        in_specs=in_specs,
        out_specs=[pl.BlockSpec((ns, 1, D_MODEL), lambda b: (b, 0, 0)), st_block],
        out_shape=[jax.ShapeDtypeStruct((bd, 1, D_MODEL), F32), jax.ShapeDtypeStruct(state.shape, F32)],
        input_output_aliases=aliases,
        compiler_params=_params("arbitrary"),
        name=f"retention_step_l{layer}",
    )(*args)


def kernel(x_prompt, x_sample, cache_kv_w128, cache_kv_w512, cache_kv_w2048, state_ret,
           norm_ffn1, ffn1_wg, ffn1_wu, ffn1_wd, norm_mix, w_in, ret_gn, w_br_a, w_br_b, w_out,
           norm_ffn2, ffn2_wg, ffn2_wu, ffn2_wd, norm_final):
    ffn1 = tuple(w.astype(BF16) for w in (ffn1_wg, ffn1_wu, ffn1_wd))
    ffn2 = tuple(w.astype(BF16) for w in (ffn2_wg, ffn2_wu, ffn2_wd))
    w_in, w_a, w_b, w_o = (w.astype(BF16) for w in (w_in, w_br_a, w_br_b, w_out))
    g_final = norm_final[None]
    batch, seq, _ = x_prompt.shape
    bd = x_sample.shape[0]

    def final_gain(l):
        return g_final if l == DEPTH - 1 else None

    caches_t = [jnp.transpose(c, (0, 1, 3, 4, 5, 2)) for c in (cache_kv_w128, cache_kv_w512, cache_kv_w2048)]
    x = x_prompt
    xs = x_sample.reshape(bd, D_MODEL)
    kv_windows, new_caches, ret_sample, ret_states = None, None, None, []
    for l in range(DEPTH):
        g_ffn1, g_mix, g_ffn2, gn = (g[l][None] for g in (norm_ffn1, norm_mix, norm_ffn2, ret_gn))
        xs = _ffn(xs, g_ffn1, *ffn1, l)
        prs = _proj(xs, g_mix, w_in, l, N_PIECES_SAMPLE, F32)
        x = _ffn(x.reshape(batch * seq, D_MODEL), g_ffn1, *ffn1, l).reshape(batch, seq, D_MODEL)
        nat, dilated, kr_t, kv_windows = _proj_prompt(x, g_mix, w_in, l, kv_windows)
        attn = [_dilated_attention(nat, (N_Q, N_K, N_V), 0)]
        attn += [_dilated_attention(dilated[g - 1], (0, 1, 2), g) for g in range(1, N_GROUPS)]
        y_r, st = _retention(x, nat, kr_t, g_mix, w_in, l, gn)
        ret_states.append(st)
        o_s, lse_s, new_caches = _cache_attention(prs, caches_t, new_caches, l)
        x = _merge(x, g_mix, w_in, w_a, w_b, w_o, l, [a[0] for a in attn], [a[1] for a in attn], y_r)
        x = _ffn(x.reshape(batch * seq, D_MODEL), g_ffn2, *ffn2, l, final_gain(l)).reshape(batch, seq, D_MODEL)
        y_rs, ret_sample = _retention_step(prs, state_ret, gn, l, ret_sample)
        xs = _merge(xs[None], g_mix, w_in, w_a, w_b, w_o, l,
                    [o_s[g].reshape(1, 1, bd, GROUP_W) for g in range(N_GROUPS)],
                    [lse_s[g].reshape(1, 1, bd, GROUP_W) for g in range(N_GROUPS)],
                    y_rs.reshape(1, bd, D_MODEL))[0]
        xs = _ffn(xs, g_ffn2, *ffn2, l, final_gain(l))
    y_prompt = x
    kv_prompt = [jnp.transpose(kv, (0, 1, 5, 2, 3, 4)) for kv in kv_windows]
    ret_prompt = jnp.stack(ret_states)
    y_sample = xs.reshape(bd, 1, D_MODEL)
    kv_sample = [jnp.transpose(c, (0, 1, 5, 2, 3, 4)) for c in new_caches]

    return (y_prompt, y_sample, kv_prompt[0], kv_sample[0], kv_prompt[1], kv_sample[1],
            kv_prompt[2], kv_sample[2], ret_prompt, ret_sample)
```
